```python
import math
import jax
import jax.numpy as jnp
from jax import lax
import numpy as np

D_MODEL = 1024
BATCH = 8
SEQ = 8192
DEPTH = 4

GRID_W = 64
CTX_LEN = 256
N_MOD = 9
FFN_DIM = 256 * ((8 * D_MODEL // 3 + 255) // 256)
SSD_P = 64
SSD_HEADS = D_MODEL // SSD_P
SSD_DI = SSD_HEADS * SSD_P
SSD_G = 2
SSD_N = 128
D_CONV = 5
CONV_CH = SSD_DI + 2 * SSD_G * SSD_N
CHUNK = 128
MLA_DN = 128
MLA_DR = 64
MLA_DV = 128
MLA_HEADS = D_MODEL // MLA_DV
MLA_Q_RANK = 3 * D_MODEL // 8
MLA_KV_RANK = D_MODEL // 4
GQA_HD = 64
GQA_HQ = D_MODEL // GQA_HD
GQA_HKV = GQA_HQ // 4
GQA_REP = GQA_HQ // GQA_HKV
WINDOW = 128
QBLK = 128
ROPE_BASE = 10000.0
EPS = 1e-6
N_EVEN = (DEPTH + 1) // 2
N_ODD = DEPTH // 2
_E0 = SSD_DI
_E1 = _E0 + CONV_CH
_E2 = _E1 + 2 * SSD_HEADS
_E3 = _E2 + MLA_Q_RANK
_E4 = _E3 + MLA_KV_RANK
E_SPLITS = [_E0, _E1, _E2, _E3, _E4]
E_IN = _E4 + MLA_DR
MIX_E = SSD_DI + MLA_HEADS * MLA_DV
O_SPLITS = [GQA_HQ * GQA_HD, GQA_HQ * GQA_HD + GQA_HKV * GQA_HD]
O_IN = GQA_HQ * GQA_HD + 2 * GQA_HKV * GQA_HD
F32 = jnp.float32

kernel_name = "hybrid_ssd_mla_swa_prefix_dit"


def rms_norm(t, gain=None):
    tf = t.astype(F32)
    y = (tf * lax.rsqrt(jnp.mean(tf * tf, axis=-1, keepdims=True) + EPS)).astype(t.dtype)
    return y if gain is None else y * gain


def modulate(t, shift, scale):
    return rms_norm(t) * (1 + scale) + shift


def swiglu(t, w_in, w_out):
    gate, up = jnp.split(t @ w_in, 2, axis=-1)
    return (jax.nn.silu(gate) * up) @ w_out


def axial_rope_tables(rows, dim):
    quarter = dim // 4
    freqs = ROPE_BASE ** (-jnp.arange(quarter, dtype=F32) / quarter)
    t = jnp.arange(rows * GRID_W)
    row = (t // GRID_W).astype(F32)
    col = (t % GRID_W).astype(F32)
    ang_r = row[:, None] * freqs
    ang_c = col[:, None] * freqs
    return (jnp.cos(ang_r), jnp.sin(ang_r), jnp.cos(ang_c), jnp.sin(ang_c))


def rope2d(t, tables):
    cos_r, sin_r, cos_c, sin_c = tables
    shp = (t.shape[1],) + (1,) * (t.ndim - 3) + (cos_r.shape[-1],)

    def rot(u, cos, sin):
        cos = cos.reshape(shp).astype(u.dtype)
        sin = sin.reshape(shp).astype(u.dtype)
        a, b = jnp.split(u, 2, axis=-1)
        return jnp.concatenate([a * cos - b * sin, b * cos + a * sin], axis=-1)

    t_r, t_c = jnp.split(t, 2, axis=-1)
    return jnp.concatenate([rot(t_r, cos_r, sin_r), rot(t_c, cos_c, sin_c)], axis=-1)


def dwconv_centered(t, w, b):
    ch = t.shape[-1]
    y = lax.conv_general_dilated(t, w[:, None, :], window_strides=(1,),
                                 padding=[(D_CONV // 2, D_CONV // 2)],
                                 dimension_numbers=("NWC", "WIO", "NWC"),
                                 feature_group_count=ch)
    return y + b


def ssd_scan(x, dt, a, bm, cm, h0, with_y):
    dtype = x.dtype
    bsz, l, h, p = x.shape
    g, n = bm.shape[2], bm.shape[3]
    r = h // g
    nc = l // CHUNK
    x = x.astype(F32).reshape(bsz, nc, CHUNK, g, r, p)
    dt = dt.astype(F32).reshape(bsz, nc, CHUNK, g, r)
    bm = bm.astype(F32).reshape(bsz, nc, CHUNK, g, n)
    cm = cm.astype(F32).reshape(bsz, nc, CHUNK, g, n)
    xdt = x * dt[..., None]
    acs = jnp.cumsum(dt * a.astype(F32).reshape(g, r), axis=2)
    states = jnp.einsum("bcsgn,bcsgr,bcsgrp->cbgrpn", bm, jnp.exp(acs[:, :, -1:] - acs), xdt)
    chunk_decay = jnp.swapaxes(jnp.exp(acs[:, :, -1]), 0, 1)

    def step(s, inp):
        st, dc = inp
        return dc[..., None, None] * s + st, s

    final, s_in = lax.scan(step, h0.astype(F32).reshape(bsz, g, r, p, n), (states, chunk_decay))
    final = final.reshape(bsz, h, p, n).astype(dtype)
    if not with_y:
        return None, final
    seg = acs[:, :, :, None] - acs[:, :, None, :]
    lower = jnp.tril(jnp.ones((CHUNK, CHUNK), bool))[:, :, None, None]
    lmat = jnp.exp(jnp.where(lower, seg, -jnp.inf))
    y = (jnp.einsum("bclgn,bcsgn,bclsgr,bcsgrp->bclgrp", cm, bm, lmat, xdt)
         + jnp.einsum("bclgn,cbgrpn,bclgr->bclgrp", cm, s_in, jnp.exp(acs)))
    return y.reshape(bsz, l, h, p).astype(dtype), final


def mla_attend(qn, qr, kn, kr, v, scale):
    bsz, m, h, _ = qn.shape
    nb = m // QBLK

    def blocks(t):
        return jnp.moveaxis(t.reshape((bsz, nb, QBLK) + t.shape[2:]), 1, 0)

    def one(qs):
        qnb, qrb = qs
        s = (jnp.einsum("bqhd,bkhd->bhqk", qnb, kn)
             + jnp.einsum("bqhd,bkd->bhqk", qrb, kr)).astype(F32) * scale
        prob = jax.nn.softmax(s, axis=-1).astype(v.dtype)
        return jnp.einsum("bhqk,bkhd->bqhd", prob, v)

    o = lax.map(one, (blocks(qn), blocks(qr)))
    return jnp.moveaxis(o, 0, 1).reshape(bsz, m, h, v.shape[-1])


def window_attend(q, k, v, kc, vc, sink, scale):
    bsz, n, hk, r, d = q.shape
    lc = kc.shape[1]
    nb = n // QBLK
    span = QBLK + 2 * WINDOW
    pad = ((0, 0), (WINDOW, WINDOW), (0, 0), (0, 0))
    kp = jnp.pad(k, pad)
    vp = jnp.pad(v, pad)
    qi = jnp.arange(QBLK)
    kj = jnp.arange(span)
    band = jnp.abs(qi[:, None] + WINDOW - kj[None, :]) <= WINDOW
    sink_l = jnp.broadcast_to(sink.astype(F32)[None, :, :, None, None], (bsz, hk, r, QBLK, 1))

    def block(i):
        start = i * QBLK
        qb = lax.dynamic_slice_in_dim(q, start, QBLK, axis=1)
        kb = lax.dynamic_slice_in_dim(kp, start, span, axis=1)
        vb = lax.dynamic_slice_in_dim(vp, start, span, axis=1)
        kpos = start - WINDOW + kj
        valid = band & ((kpos >= 0) & (kpos < n))[None, :]
        s_loc = jnp.einsum("bqhrd,bkhd->bhrqk", qb, kb).astype(F32) * scale
        s_loc = jnp.where(valid, s_loc, -jnp.inf)
        s_ctx = jnp.einsum("bqhrd,bkhd->bhrqk", qb, kc).astype(F32) * scale
        prob = jax.nn.softmax(jnp.concatenate([s_loc, s_ctx, sink_l], axis=-1), axis=-1).astype(v.dtype)
        return (jnp.einsum("bhrqk,bkhd->bqhrd", prob[..., :span], vb)
                + jnp.einsum("bhrqk,bkhd->bqhrd", prob[..., span:span + lc], vc))

    o = lax.map(block, jnp.arange(nb))
    return jnp.moveaxis(o, 0, 1).reshape(bsz, n, hk, r, d)


def sink_attend(q, k, v, sink, scale):
    bsz, m, hk, r, _ = q.shape
    s = jnp.einsum("bqhrd,bkhd->bhrqk", q, k).astype(F32) * scale
    sink_l = jnp.broadcast_to(sink.astype(F32)[None, :, :, None, None], (bsz, hk, r, m, 1))
    prob = jax.nn.softmax(jnp.concatenate([s, sink_l], axis=-1), axis=-1)[..., :-1].astype(v.dtype)
    return jnp.einsum("bhrqk,bkhd->bqhrd", prob, v)


def even_mixer(u, uc, w_in, conv_w, conv_b, dt_bias, a_log, d_skip, ssd_norm, qa_norm, w_qb,
               kv_norm, w_kvb, qn_g, qr_g, kn_g, kr_g, w_out, rope, need_ctx):
    bsz = u.shape[0]
    a_fwd = -jnp.exp(a_log[0].astype(F32))
    a_bwd = -jnp.exp(a_log[1].astype(F32))

    def project(t):
        m = t.shape[1]
        z, xbc, dtr, qa, ckv, kr = jnp.split(t @ w_in, E_SPLITS, axis=-1)
        xbc = jax.nn.silu(dwconv_centered(xbc, conv_w, conv_b))
        xs, bm, cm = jnp.split(xbc, [SSD_DI, SSD_DI + SSD_G * SSD_N], axis=-1)
        xs = xs.reshape(bsz, m, SSD_HEADS, SSD_P)
        bm = bm.reshape(bsz, m, SSD_G, SSD_N)
        cm = cm.reshape(bsz, m, SSD_G, SSD_N)
        dt_f = jax.nn.softplus(dtr[..., :SSD_HEADS] + dt_bias[0])
        dt_b = jax.nn.softplus(dtr[..., SSD_HEADS:] + dt_bias[1])
        kv = (rms_norm(ckv, kv_norm) @ w_kvb).reshape(bsz, m, MLA_HEADS, MLA_DN + MLA_DV)
        kn, v = jnp.split(kv, [MLA_DN], axis=-1)
        return z, xs, bm, cm, dt_f, dt_b, qa, rms_norm(kn, kn_g), rms_norm(kr, kr_g), v

    def queries(qa):
        q = (rms_norm(qa, qa_norm) @ w_qb).reshape(bsz, qa.shape[1], MLA_HEADS, MLA_DN + MLA_DR)
        qn, qr = jnp.split(q, [MLA_DN], axis=-1)
        return rms_norm(qn, qn_g), rms_norm(qr, qr_g)

    def ssd_merge(y_f, y_b_rev, xs, z):
        m = xs.shape[1]
        y = (y_f + jnp.flip(y_b_rev, 1) + d_skip[:, None] * xs).reshape(bsz, m, SSD_DI)
        y = rms_norm((y * jax.nn.silu(z)).reshape(bsz, m, SSD_G, SSD_DI // SSD_G))
        return y.reshape(bsz, m, SSD_DI) * ssd_norm

    def flip(t):
        return jnp.flip(t, 1)

    zc, xc, bc, cc, dfc, dbc, qac, knc, krc, vc = project(uc)
    z, xs, bm, cm, df, db, qa, kn, kr, v = project(u)
    h0 = jnp.zeros((bsz, SSD_HEADS, SSD_P, SSD_N), u.dtype)
    yfc, s_fc = ssd_scan(xc, dfc, a_fwd, bc, cc, h0, need_ctx)
    ybc, s_bc = ssd_scan(flip(xc), flip(dbc), a_bwd, flip(bc), flip(cc), h0, need_ctx)
    yf, _ = ssd_scan(xs, df, a_fwd, bm, cm, s_fc, True)
    yb, _ = ssd_scan(flip(xs), flip(db), a_bwd, flip(bm), flip(cm), s_bc, True)
    scale = (MLA_DN + MLA_DR) ** -0.5
    qn, qr = queries(qa)
    o = mla_attend(qn, rope2d(qr, rope),
                   jnp.concatenate([kn, knc], axis=1),
                   jnp.concatenate([rope2d(kr, rope), krc], axis=1),
                   jnp.concatenate([v, vc], axis=1), scale)
    y = jnp.concatenate([ssd_merge(yf, yb, xs, z),
                         o.reshape(bsz, -1, MLA_HEADS * MLA_DV)], axis=-1) @ w_out
    yc = None
    if need_ctx:
        qnc, qrc = queries(qac)
        oc = mla_attend(qnc, qrc, knc, krc, vc, scale)
        yc = jnp.concatenate([ssd_merge(yfc, ybc, xc, zc),
                              oc.reshape(bsz, -1, MLA_HEADS * MLA_DV)], axis=-1) @ w_out
    return y, yc


def odd_mixer(u, uc, w_in, q_g, k_g, sink, w_out, rope, need_ctx):
    bsz = u.shape[0]
    sink_hr = sink.reshape(GQA_HKV, GQA_REP)
    scale = GQA_HD ** -0.5

    def project(t):
        m = t.shape[1]
        q, k, v = jnp.split(t @ w_in, O_SPLITS, axis=-1)
        q = q.reshape(bsz, m, GQA_HKV, GQA_REP, GQA_HD)
        k = rms_norm(k.reshape(bsz, m, GQA_HKV, GQA_HD), k_g)
        return q, k, v.reshape(bsz, m, GQA_HKV, GQA_HD)

    qc, kc, vc = project(uc)
    q, k, v = project(u)
    q = rope2d(rms_norm(q, q_g), rope)
    k = rope2d(k, rope)
    y = window_attend(q, k, v, kc, vc, sink_hr, scale).reshape(bsz, -1, GQA_HQ * GQA_HD) @ w_out
    yc = None
    if need_ctx:
        oc = sink_attend(rms_norm(qc, q_g), kc, vc, sink_hr, scale)
        yc = oc.reshape(bsz, -1, GQA_HQ * GQA_HD) @ w_out
    return y, yc


def setup_inputs(seed: int = 0) -> dict:
    key = jax.random.key(seed)
    keys = list(jax.random.split(key, 40))
    D = D_MODEL

    def normal(shape):
        return jax.random.normal(keys.pop(), shape, F32)

    def w(shape, fan_in, s=1.0):
        return (s * fan_in ** -0.5) * normal(shape)

    def gain(shape):
        return 1.0 + 0.02 * normal(shape)

    x = normal((BATCH, SEQ, D))
    c = normal((BATCH, D))
    ctx = normal((BATCH, CTX_LEN, D))
    c_ctx = normal((D,))
    w_mod = w((DEPTH, D, N_MOD * D), D, 0.5)
    b_mod = 0.02 * normal((DEPTH, N_MOD * D))
    w_ff1_in = w((DEPTH, D, 2 * FFN_DIM), D)
    w_ff1_out = w((DEPTH, FFN_DIM, D), FFN_DIM)
    w_ff2_in = w((DEPTH, D, 2 * FFN_DIM), D)
    w_ff2_out = w((DEPTH, FFN_DIM, D), FFN_DIM)
    w_in_e = w((N_EVEN, D, E_IN), D)
    conv_w = w((N_EVEN, D_CONV, CONV_CH), D_CONV)
    conv_b = 0.02 * normal((N_EVEN, CONV_CH))
    u_dt = jax.random.uniform(keys.pop(), (N_EVEN, 2, SSD_HEADS), F32)
    dt0 = jnp.exp(u_dt * (math.log(0.1) - math.log(0.001)) + math.log(0.001))
    dt_bias = dt0 + jnp.log(-jnp.expm1(-dt0))
    a_log = jnp.log(jax.random.uniform(keys.pop(), (N_EVEN, 2, SSD_HEADS), F32, 1.0, 16.0))
    d_skip = gain((N_EVEN, SSD_HEADS))
    ssd_norm = gain((N_EVEN, SSD_DI))
    mla_qa_norm = gain((N_EVEN, MLA_Q_RANK))
    w_qb = w((N_EVEN, MLA_Q_RANK, MLA_HEADS * (MLA_DN + MLA_DR)), MLA_Q_RANK)
    mla_kv_norm = gain((N_EVEN, MLA_KV_RANK))
    w_kvb = w((N_EVEN, MLA_KV_RANK, MLA_HEADS * (MLA_DN + MLA_DV)), MLA_KV_RANK)
    mla_qn_norm = gain((N_EVEN, MLA_DN))
    mla_qr_norm = gain((N_EVEN, MLA_DR))
    mla_kn_norm = gain((N_EVEN, MLA_DN))
    mla_kr_norm = gain((N_EVEN, MLA_DR))
    w_out_e = w((N_EVEN, MIX_E, D), MIX_E)
    w_in_o = w((N_ODD, D, O_IN), D)
    gqa_q_norm = gain((N_ODD, GQA_HD))
    gqa_k_norm = gain((N_ODD, GQA_HD))
    sink = 0.5 * normal((N_ODD, GQA_HQ))
    w_out_o = w((N_ODD, GQA_HQ * GQA_HD, D), GQA_HQ * GQA_HD)
    return {"x": x, "c": c, "ctx": ctx, "c_ctx": c_ctx, "w_mod": w_mod, "b_mod": b_mod,
            "w_ff1_in": w_ff1_in, "w_ff1_out": w_ff1_out, "w_ff2_in": w_ff2_in, "w_ff2_out": w_ff2_out,
            "w_in_e": w_in_e, "conv_w": conv_w, "conv_b": conv_b, "dt_bias": dt_bias, "a_log": a_log,
            "d_skip": d_skip, "ssd_norm": ssd_norm, "mla_qa_norm": mla_qa_norm, "w_qb": w_qb,
            "mla_kv_norm": mla_kv_norm, "w_kvb": w_kvb, "mla_qn_norm": mla_qn_norm,
            "mla_qr_norm": mla_qr_norm, "mla_kn_norm": mla_kn_norm, "mla_kr_norm": mla_kr_norm,
            "w_out_e": w_out_e, "w_in_o": w_in_o, "gqa_q_norm": gqa_q_norm, "gqa_k_norm": gqa_k_norm,
            "sink": sink, "w_out_o": w_out_o}


def reference(x, c, ctx, c_ctx, w_mod, b_mod, w_ff1_in, w_ff1_out, w_ff2_in, w_ff2_out,
              w_in_e, conv_w, conv_b, dt_bias, a_log, d_skip, ssd_norm, mla_qa_norm, w_qb,
              mla_kv_norm, w_kvb, mla_qn_norm, mla_qr_norm, mla_kn_norm, mla_kr_norm, w_out_e,
              w_in_o, gqa_q_norm, gqa_k_norm, sink, w_out_o):
    bsz, n, _ = x.shape
    rows = n // GRID_W
    rope_mla = axial_rope_tables(rows, MLA_DR)
    rope_gqa = axial_rope_tables(rows, GQA_HD)
    sc = jax.nn.silu(c)
    scc = jax.nn.silu(c_ctx)
    h, hc = x, ctx
    for l in range(DEPTH):
        need_ctx = l < DEPTH - 1
        m = (sc @ w_mod[l] + b_mod[l]).reshape(bsz, N_MOD, 1, D_MODEL)
        mc = (scc @ w_mod[l] + b_mod[l]).reshape(N_MOD, D_MODEL)
        h = h + 0.5 * m[:, 2] * swiglu(modulate(h, m[:, 0], m[:, 1]), w_ff1_in[l], w_ff1_out[l])
        hc = hc + 0.5 * mc[2] * swiglu(modulate(hc, mc[0], mc[1]), w_ff1_in[l], w_ff1_out[l])
        u = modulate(h, m[:, 3], m[:, 4])
        uc = modulate(hc, mc[3], mc[4])
        if l % 2 == 0:
            e = l // 2
            y, yc = even_mixer(u, uc, w_in_e[e], conv_w[e], conv_b[e], dt_bias[e], a_log[e],
                               d_skip[e], ssd_norm[e], mla_qa_norm[e], w_qb[e], mla_kv_norm[e],
                               w_kvb[e], mla_qn_norm[e], mla_qr_norm[e], mla_kn_norm[e],
                               mla_kr_norm[e], w_out_e[e], rope_mla, need_ctx)
        else:
            o = l // 2
            y, yc = odd_mixer(u, uc, w_in_o[o], gqa_q_norm[o], gqa_k_norm[o], sink[o],
                              w_out_o[o], rope_gqa, need_ctx)
        h = h + m[:, 5] * y
        h = h + 0.5 * m[:, 8] * swiglu(modulate(h, m[:, 6], m[:, 7]), w_ff2_in[l], w_ff2_out[l])
        if need_ctx:
            hc = hc + mc[5] * yc
            hc = hc + 0.5 * mc[8] * swiglu(modulate(hc, mc[6], mc[7]), w_ff2_in[l], w_ff2_out[l])
    return h
```

```python
import functools
import math

import jax
import jax.numpy as jnp
import numpy as np
from jax import lax
from jax.experimental import pallas as pl
from jax.experimental.pallas import tpu as pltpu

F32 = jnp.float32
BF16 = jnp.bfloat16
HIGHEST = lax.Precision.HIGHEST

D_MODEL = 1024
GRID_W = 64
N_MOD = 9
FFN_DIM = 2816
SSD_P = 64
SSD_HEADS = 16
SSD_DI = 1024
SSD_G = 2
SSD_N = 128
D_CONV = 5
CONV_CH = SSD_DI + 2 * SSD_G * SSD_N
CHUNK = 128
MLA_DN = 128
MLA_DR = 64
MLA_DV = 128
MLA_HEADS = 8
MLA_Q_RANK = 384
MLA_KV_RANK = 256
GQA_HD = 64
GQA_HQ = 16
GQA_HKV = 4
GQA_REP = 4
WINDOW = 128
ROPE_BASE = 10000.0
EPS = 1e-6
_E0 = SSD_DI
_E1 = _E0 + CONV_CH
_E2 = _E1 + 2 * SSD_HEADS
_E3 = _E2 + MLA_Q_RANK
_E4 = _E3 + MLA_KV_RANK
E_IN = _E4 + MLA_DR
MLA_SCALE = (MLA_DN + MLA_DR) ** -0.5
GQA_SCALE = GQA_HD ** -0.5

LANES = 128
SUBLANES = 8
VMEM_LIMIT_BYTES = 56 * 1024 * 1024
MOD_ROWS = 16
FFN_CHUNK = 256
N_FFN_CHUNKS = FFN_DIM // FFN_CHUNK
TOKEN_TILE = 512
FLASH_TQ = 512
FLASH_TK = 512
HALO = SUBLANES


def _params(*sem):
    return pltpu.CompilerParams(dimension_semantics=sem, vmem_limit_bytes=VMEM_LIMIT_BYTES)


def _const_spec(shape):
    nd = len(shape)
    return pl.BlockSpec(shape, lambda *_: (0,) * nd)


def _dot(a, b):
    return jnp.dot(a, b, preferred_element_type=F32)


def _dot_nt(a, b):
    return lax.dot_general(a, b, (((1,), (1,)), ((), ())), preferred_element_type=F32)


def _silu(x):
    return x * jax.nn.sigmoid(x)


def _rms(x):
    return x * lax.rsqrt(jnp.mean(x * x, axis=-1, keepdims=True) + EPS)


def _modulate(h, shift, scale):
    return _rms(h) * (1.0 + scale) + shift


def _lane_lo(shape):
    return (lax.broadcasted_iota(jnp.int32, shape, len(shape) - 1) % LANES) < (LANES // 2)


def _mod_kernel(s_ref, w_ref, b_ref, o_ref):
    s = _silu(s_ref[...])
    o_ref[0] = jnp.dot(s, w_ref[0], preferred_element_type=F32, precision=HIGHEST) + b_ref[0]


def _mod_vectors(cond, w_mod, b_mod):
    depth, d, nm = w_mod.shape
    tn = 1024
    return pl.pallas_call(
        _mod_kernel,
        grid=(depth, nm // tn),
        in_specs=[
            pl.BlockSpec((MOD_ROWS, d), lambda l, j: (0, 0)),
            pl.BlockSpec((1, d, tn), lambda l, j: (l, 0, j)),
            pl.BlockSpec((1, 1, tn), lambda l, j: (l, 0, j)),
        ],
        out_specs=pl.BlockSpec((1, MOD_ROWS, tn), lambda l, j: (l, 0, j)),
        out_shape=jax.ShapeDtypeStruct((depth, MOD_ROWS, nm), F32),
        compiler_params=_params("arbitrary", "arbitrary"),
        name="mod_vectors",
    )(cond, w_mod, b_mod.reshape(depth, 1, nm))


def _ffn_kernel(h_ref, m_ref, wg_ref, wu_ref, wo_ref, o_ref, a_ref, *, k0):
    h = h_ref[0]
    xm = _modulate(h, m_ref[0, k0:k0 + 1, :], m_ref[0, k0 + 1:k0 + 2, :]).astype(BF16)
    for j in range(N_FFN_CHUNKS):
        g = _dot(xm, wg_ref[j])
        u = _dot(xm, wu_ref[j])
        a_ref[:, j * FFN_CHUNK:(j + 1) * FFN_CHUNK] = (_silu(g) * u).astype(BF16)
    y = _dot(a_ref[...], wo_ref[...])
    o_ref[0] = h + (0.5 * m_ref[0, k0 + 2:k0 + 3, :]) * y


def _token_tile(n):
    return min(TOKEN_TILE, n)


def _ffn(h, mod_l, row_of, wg, wu, wo, k0):
    bsz, n, d = h.shape
    tm = _token_tile(n)
    return pl.pallas_call(
        functools.partial(_ffn_kernel, k0=k0),
        grid=(bsz, n // tm),
        in_specs=[
            pl.BlockSpec((1, tm, d), lambda b, i: (b, i, 0)),
            pl.BlockSpec((1, N_MOD, d), lambda b, i: (row_of(b), 0, 0)),
            _const_spec(wg.shape),
            _const_spec(wu.shape),
            _const_spec(wo.shape),
        ],
        out_specs=pl.BlockSpec((1, tm, d), lambda b, i: (b, i, 0)),
        out_shape=jax.ShapeDtypeStruct(h.shape, F32),
        scratch_shapes=[pltpu.VMEM((tm, FFN_DIM), BF16)],
        compiler_params=_params("arbitrary", "arbitrary"),
        name="ffn",
    )(h, mod_l, wg, wu, wo)


def _outproj_kernel(*refs, n_in):
    h_ref, m_ref = refs[0], refs[1]
    a_refs = refs[2:2 + n_in]
    w_refs = refs[2 + n_in:2 + 2 * n_in]
    o_ref = refs[2 + 2 * n_in]
    acc = _dot(a_refs[0][0], w_refs[0][...])
    for a_ref, w_ref in zip(a_refs[1:], w_refs[1:]):
        acc = acc + _dot(a_ref[0], w_ref[...])
    o_ref[0] = h_ref[0] + m_ref[0, 5:6, :] * acc


def _outproj(h, mod_l, row_of, acts, weights):
    bsz, n, d = h.shape
    tm = _token_tile(n)
    n_in = len(acts)
    return pl.pallas_call(
        functools.partial(_outproj_kernel, n_in=n_in),
        grid=(bsz, n // tm),
        in_specs=[
            pl.BlockSpec((1, tm, d), lambda b, i: (b, i, 0)),
            pl.BlockSpec((1, N_MOD, d), lambda b, i: (row_of(b), 0, 0)),
        ] + [pl.BlockSpec((1, tm, a.shape[-1]), lambda b, i: (b, i, 0)) for a in acts]
        + [_const_spec(w.shape) for w in weights],
        out_specs=pl.BlockSpec((1, tm, d), lambda b, i: (b, i, 0)),
        out_shape=jax.ShapeDtypeStruct(h.shape, F32),
        compiler_params=_params("arbitrary", "arbitrary"),
        name="outproj",
    )(h, mod_l, *acts, *weights)


def _norm_rope_pair(t2, g2, tab):
    lo = _lane_lo(t2.shape)
    ss = jnp.sum(jnp.where(lo, t2 * t2, 0.0), axis=-1, keepdims=True) * (1.0 / MLA_DR)
    y = t2 * lax.rsqrt(ss + EPS) * g2 * tab
    return y + pltpu.roll(y, LANES // 2, 1)


def _even_proj_kernel(h_ref, m_ref, tab_ref, wz_ref, wx_ref, wm_ref, wqa_ref, wkv_ref, wqn_ref,
                      wqr_ref, wkn_ref, wv_ref, gqa_ref, gkv_ref, gqn_ref, gqr_ref, gkn_ref,
                      gkr_ref, z_ref, xbc_ref, dt_ref, q_ref, k_ref, v_ref):
    u = _modulate(h_ref[0], m_ref[0, 3:4, :], m_ref[0, 4:5, :]).astype(BF16)
    tab = tab_ref[...]
    z_ref[0] = _dot(u, wz_ref[...])
    xbc_ref[0] = _dot(u, wx_ref[...])
    misc = _dot(u, wm_ref[...])
    dt_ref[0] = misc[:, :LANES]
    kr = _norm_rope_pair(misc[:, LANES:], gkr_ref[...], tab)[:, :MLA_DR].astype(BF16)
    qa = (_rms(_dot(u, wqa_ref[...])) * gqa_ref[...]).astype(BF16)
    qn = _dot(qa, wqn_ref[...])
    qr2 = _dot(qa, wqr_ref[...])
    ckv = (_rms(_dot(u, wkv_ref[...])) * gkv_ref[...]).astype(BF16)
    kn = _dot(ckv, wkn_ref[...])
    v = _dot(ckv, wv_ref[...])
    for hd in range(MLA_HEADS):
        sl = slice(hd * LANES, (hd + 1) * LANES)
        q_ref[0, hd, :, 0:MLA_DN] = (_rms(qn[:, sl]) * gqn_ref[...] * MLA_SCALE).astype(BF16)
        qr = _norm_rope_pair(qr2[:, sl], gqr_ref[...], tab)[:, :MLA_DR]
        q_ref[0, hd, :, MLA_DN:MLA_DN + MLA_DR] = (qr * MLA_SCALE).astype(BF16)
        k_ref[0, hd, :, 0:MLA_DN] = (_rms(kn[:, sl]) * gkn_ref[...]).astype(BF16)
        k_ref[0, hd, :, MLA_DN:MLA_DN + MLA_DR] = kr
        v_ref[0, hd] = v[:, sl].astype(BF16)


def _even_proj(h, mod_l, row_of, tab, w):
    bsz, n, d = h.shape
    tm = _token_tile(n)
    weights = [w["wz"], w["wx"], w["wm"], w["wqa"], w["wkv"], w["wqn"], w["wqr"], w["wkn"], w["wv"],
               w["gqa"], w["gkv"], w["gqn"], w["gqr"], w["gkn"], w["gkr"]]
    dqk = MLA_DN + MLA_DR
    tok = lambda width: pl.BlockSpec((1, tm, width), lambda b, i: (b, i, 0))
    head = lambda width: pl.BlockSpec((1, MLA_HEADS, tm, width), lambda b, i: (b, 0, i, 0))
    return pl.pallas_call(
        _even_proj_kernel,
        grid=(bsz, n // tm),
        in_specs=[
            tok(d),
            pl.BlockSpec((1, N_MOD, d), lambda b, i: (row_of(b), 0, 0)),
            pl.BlockSpec((tm, LANES), lambda b, i: (i, 0)),
        ] + [_const_spec(x.shape) for x in weights],
        out_specs=[tok(SSD_DI), tok(CONV_CH), tok(LANES), head(dqk), head(dqk), head(MLA_DV)],
        out_shape=[
            jax.ShapeDtypeStruct((bsz, n, SSD_DI), F32),
            jax.ShapeDtypeStruct((bsz, n, CONV_CH), F32),
            jax.ShapeDtypeStruct((bsz, n, LANES), F32),
            jax.ShapeDtypeStruct((bsz, MLA_HEADS, n, dqk), BF16),
            jax.ShapeDtypeStruct((bsz, MLA_HEADS, n, dqk), BF16),
            jax.ShapeDtypeStruct((bsz, MLA_HEADS, n, MLA_DV), BF16),
        ],
        compiler_params=_params("arbitrary", "arbitrary"),
        name="even_proj",
    )(h, mod_l, tab, *weights)


def _ssd_kernel(*refs, nc, reverse, lane0):
    if reverse:
        (xm_ref, xp_ref, xn_ref, dt_ref, cw_ref, cb_ref, dtb_ref, a_ref, h0_ref, yf_ref, z_ref,
         dsk_ref, nrm_ref, y_ref, st_ref, ext_ref, gat_ref) = refs
    else:
        (xm_ref, xp_ref, xn_ref, dt_ref, cw_ref, cb_ref, dtb_ref, a_ref, h0_ref,
         y_ref, st_ref, ext_ref) = refs
    c = pl.program_id(1)
    chunk = (nc - 1 - c) if reverse else c

    @pl.when(c == 0)
    def _():
        st_ref[...] = h0_ref[...]

    ext_ref[0:HALO, :] = jnp.where(chunk == 0, 0.0, xp_ref[0])
    ext_ref[HALO:HALO + CHUNK, :] = xm_ref[0]
    ext_ref[HALO + CHUNK:2 * HALO + CHUNK, :] = jnp.where(chunk == nc - 1, 0.0, xn_ref[0])
    acc = cb_ref[...]
    for k in range(D_CONV):
        off = HALO - D_CONV // 2 + k
        acc = acc + cw_ref[k:k + 1, :] * ext_ref[off:off + CHUNK, :]
    xbc = _silu(acc)
    xs = xbc[:, :SSD_DI]

    xr = dt_ref[0] + dtb_ref[...]
    dt = jnp.maximum(xr, 0.0) + jnp.log1p(jnp.exp(-jnp.abs(xr)))
    dta = dt * a_ref[...]
    ri = lax.broadcasted_iota(jnp.int32, (CHUNK, CHUNK), 0)
    ci = lax.broadcasted_iota(jnp.int32, (CHUNK, CHUNK), 1)
    tri = (ci >= ri) if reverse else (ri >= ci)
    acs = jnp.dot(tri.astype(F32), dta, preferred_element_type=F32, precision=HIGHEST)
    acs_t = acs.T
    dt_t = dt.T
    edge = 0 if reverse else CHUNK - 1
    lo = _lane_lo((CHUNK, LANES))
    lo_row = _lane_lo((1, LANES))

    cb_g, bt_g, c_g = [], [], []
    for g in range(SSD_G):
        bg = xbc[:, SSD_DI + g * SSD_N:SSD_DI + (g + 1) * SSD_N]
        cg = xbc[:, SSD_DI + SSD_G * SSD_N + g * SSD_N:SSD_DI + SSD_G * SSD_N + (g + 1) * SSD_N]
        cgb = cg.astype(BF16)
        cb_g.append(_dot_nt(cgb, bg.astype(BF16)))
        bt_g.append(bg.T)
        c_g.append(cgb)

    def head_terms(j, g):
        col = acs[:, j:j + 1]
        row = acs_t[j:j + 1, :]
        dtrow = dt_t[j:j + 1, :]
        lmat = jnp.exp(jnp.where(tri, col - row, -jnp.inf))
        gmat = (cb_g[g] * lmat * dtrow).astype(BF16)
        tot = acs_t[j:j + 1, edge:edge + 1]
        bw = (bt_g[g] * (jnp.exp(tot - row) * dtrow)).astype(BF16)
        return gmat, bw, jnp.exp(col), jnp.exp(tot)

    for k in range(SSD_HEADS // 2):
        g = (2 * k) // (SSD_HEADS // SSD_G)
        xpair = xs[:, k * LANES:(k + 1) * LANES]
        rhs = jnp.concatenate([jnp.where(lo, xpair, 0.0), jnp.where(lo, 0.0, xpair)], axis=0).astype(BF16)
        ga, bwa, ea, da = head_terms(lane0 + 2 * k, g)
        gb, bwb, eb, db = head_terms(lane0 + 2 * k + 1, g)
        s_in = st_ref[0, k]
        y = (_dot(jnp.concatenate([ga, gb], axis=1), rhs)
             + _dot(c_g[g], s_in.astype(BF16)) * jnp.where(lo, ea, eb))
        st_ref[0, k] = s_in * jnp.where(lo_row, da, db) + _dot(jnp.concatenate([bwa, bwb], axis=1), rhs)
        sl = slice(k * LANES, (k + 1) * LANES)
        if reverse:
            zt = z_ref[0, :, sl]
            gat_ref[:, sl] = (yf_ref[0, :, sl] + y + dsk_ref[:, sl] * xpair) * _silu(zt)
        else:
            y_ref[0, :, sl] = y

    if reverse:
        gw = SSD_DI // SSD_G
        for g in range(SSD_G):
            seg = gat_ref[:, g * gw:(g + 1) * gw]
            y_ref[0, :, g * gw:(g + 1) * gw] = (_rms(seg) * nrm_ref[:, g * gw:(g + 1) * gw]).astype(BF16)


def _ssd(xbc, dtm, h0, w, reverse, merge=None):
    bsz, n, _ = xbc.shape
    nc = n // CHUNK
    per = CHUNK // HALO
    nhb = n // HALO
    cidx = (lambda c: nc - 1 - c) if reverse else (lambda c: c)
    tok = lambda width: pl.BlockSpec((1, CHUNK, width), lambda b, c: (b, cidx(c), 0))
    state_spec = pl.BlockSpec((1, SSD_HEADS // 2, SSD_N, LANES), lambda b, c: (b, 0, 0, 0))
    in_specs = [
        tok(CONV_CH),
        pl.BlockSpec((1, HALO, CONV_CH), lambda b, c: (b, jnp.maximum(cidx(c) * per - 1, 0), 0)),
        pl.BlockSpec((1, HALO, CONV_CH), lambda b, c: (b, jnp.minimum((cidx(c) + 1) * per, nhb - 1), 0)),
        tok(LANES),
        _const_spec(w["conv_w"].shape),
        _const_spec(w["conv_b"].shape),
        _const_spec(w["dt_bias"].shape),
        _const_spec(w["a_rev" if reverse else "a_fwd"].shape),
        state_spec,
    ]
    args = [xbc, xbc, xbc, dtm, w["conv_w"], w["conv_b"], w["dt_bias"], w["a_rev" if reverse else "a_fwd"], h0]
    scratch = [pltpu.VMEM((CHUNK + 2 * HALO, CONV_CH), F32)]
    if reverse:
        y_fwd, z = merge
        in_specs += [tok(SSD_DI), tok(SSD_DI), _const_spec(w["d_skip"].shape), _const_spec(w["ssd_norm"].shape)]
        args += [y_fwd, z, w["d_skip"], w["ssd_norm"]]
        scratch.append(pltpu.VMEM((CHUNK, SSD_DI), F32))
    y_dtype = BF16 if reverse else F32
    return pl.pallas_call(
        functools.partial(_ssd_kernel, nc=nc, reverse=reverse, lane0=SSD_HEADS if reverse else 0),
        grid=(bsz, nc),
        in_specs=in_specs,
        out_specs=[tok(SSD_DI), state_spec],
        out_shape=[jax.ShapeDtypeStruct((bsz, n, SSD_DI), y_dtype), jax.ShapeDtypeStruct(h0.shape, F32)],
        scratch_shapes=scratch,
        compiler_params=_params("arbitrary", "arbitrary"),
        name="ssd_rev" if reverse else "ssd_fwd",
    )(*args)


def _flash_kernel(*refs, n_kv, tk, has_ctx):
    if has_ctx:
        q_ref, k_ref, v_ref, kc_ref, vc_ref, o_ref, m_ref, l_ref, acc_ref = refs
    else:
        q_ref, k_ref, v_ref, o_ref, m_ref, l_ref, acc_ref = refs
    q = q_ref[0, 0]
    m_ref[...] = jnp.full(m_ref.shape, -jnp.inf, F32)
    l_ref[...] = jnp.zeros(l_ref.shape, F32)
    acc_ref[...] = jnp.zeros(acc_ref.shape, F32)

    def update(kb, vb):
        s = _dot_nt(q, kb)
        m_old = m_ref[...]
        m_new = jnp.maximum(m_old, jnp.max(s, axis=-1, keepdims=True))
        alpha = jnp.exp(m_old - m_new)
        p = jnp.exp(s - m_new)
        l_ref[...] = alpha * l_ref[...] + jnp.sum(p, axis=-1, keepdims=True)
        acc_ref[...] = alpha * acc_ref[...] + _dot(p.astype(BF16), vb)
        m_ref[...] = m_new

    def body(j, carry):
        start = pl.multiple_of(j * tk, tk)
        update(k_ref[0, 0, pl.ds(start, tk), :], v_ref[0, 0, pl.ds(start, tk), :])
        return carry

    lax.fori_loop(0, n_kv // tk, body, 0)
    if has_ctx:
        update(kc_ref[0, 0], vc_ref[0, 0])
    o_ref[0] = (acc_ref[...] / l_ref[...]).astype(BF16)


def _flash(q, k, v, ctx_kv=None):
    bsz, nh, n, dqk = q.shape
    n_kv = k.shape[2]
    dv = v.shape[3]
    tq = min(FLASH_TQ, n)
    tk = min(FLASH_TK, n_kv)
    has_ctx = ctx_kv is not None
    kv_spec = lambda rows, width: pl.BlockSpec((1, 1, rows, width), lambda b, h, i: (b, h, 0, 0))
    in_specs = [pl.BlockSpec((1, 1, tq, dqk), lambda b, h, i: (b, h, i, 0)), kv_spec(n_kv, dqk), kv_spec(n_kv, dv)]
    args = [q, k, v]
    if has_ctx:
        kc, vc = ctx_kv
        in_specs += [kv_spec(kc.shape[2], dqk), kv_spec(vc.shape[2], dv)]
        args += [kc, vc]
    return pl.pallas_call(
        functools.partial(_flash_kernel, n_kv=n_kv, tk=tk, has_ctx=has_ctx),
        grid=(bsz, nh, n // tq),
        in_specs=in_specs,
        out_specs=pl.BlockSpec((1, tq, dv), lambda b, h, i: (b, i, h)),
        out_shape=jax.ShapeDtypeStruct((bsz, n, nh * dv), BF16),
        scratch_shapes=[pltpu.VMEM((tq, 1), F32), pltpu.VMEM((tq, 1), F32), pltpu.VMEM((tq, dv), F32)],
        compiler_params=_params("arbitrary", "arbitrary", "arbitrary"),
        name="mla_flash",
    )(*args)


def _odd_proj_kernel(h_ref, m_ref, cos_ref, sin_ref, wq_ref, wk_ref, wv_ref, gq_ref, gk_ref,
                     q_ref, k_ref, v_ref):
    u = _modulate(h_ref[0], m_ref[0, 3:4, :], m_ref[0, 4:5, :]).astype(BF16)
    cos = cos_ref[...]
    sin = sin_ref[...]
    lo = _lane_lo(cos.shape)
    first_quarter = (lax.broadcasted_iota(jnp.int32, cos.shape, 1) % (GQA_HD // 2)) < (GQA_HD // 4)

    def norm_rope(x2, g2):
        sq = x2 * x2
        s_lo = jnp.sum(jnp.where(lo, sq, 0.0), axis=-1, keepdims=True) * (1.0 / GQA_HD)
        s_hi = jnp.sum(jnp.where(lo, 0.0, sq), axis=-1, keepdims=True) * (1.0 / GQA_HD)
        y = x2 * jnp.where(lo, lax.rsqrt(s_lo + EPS), lax.rsqrt(s_hi + EPS)) * g2
        rot = jnp.where(first_quarter, pltpu.roll(y, LANES - GQA_HD // 4, 1), pltpu.roll(y, GQA_HD // 4, 1))
        return y * cos + rot * sin

    q = _dot(u, wq_ref[...])
    k = _dot(u, wk_ref[...])
    for j in range(q.shape[1] // LANES):
        sl = slice(j * LANES, (j + 1) * LANES)
        q_ref[0, :, sl] = (norm_rope(q[:, sl], gq_ref[...]) * GQA_SCALE).astype(BF16)
    for j in range(k.shape[1] // LANES):
        sl = slice(j * LANES, (j + 1) * LANES)
        k_ref[0, :, sl] = norm_rope(k[:, sl], gk_ref[...]).astype(BF16)
    v_ref[0] = _dot(u, wv_ref[...]).astype(BF16)


def _odd_proj(h, mod_l, row_of, cos, sin, w):
    bsz, n, d = h.shape
    tm = _token_tile(n)
    weights = [w["wq"], w["wk"], w["wv"], w["gq"], w["gk"]]
    tok = lambda width: pl.BlockSpec((1, tm, width), lambda b, i: (b, i, 0))
    nq = GQA_HQ * GQA_HD
    nkv = GQA_HKV * GQA_HD
    return pl.pallas_call(
        _odd_proj_kernel,
        grid=(bsz, n // tm),
        in_specs=[
            tok(d),
            pl.BlockSpec((1, N_MOD, d), lambda b, i: (row_of(b), 0, 0)),
            pl.BlockSpec((tm, LANES), lambda b, i: (i, 0)),
            pl.BlockSpec((tm, LANES), lambda b, i: (i, 0)),
        ] + [_const_spec(x.shape) for x in weights],
        out_specs=[tok(nq), tok(nkv), tok(nkv)],
        out_shape=[
            jax.ShapeDtypeStruct((bsz, n, nq), BF16),
            jax.ShapeDtypeStruct((bsz, n, nkv), BF16),
            jax.ShapeDtypeStruct((bsz, n, nkv), BF16),
        ],
        compiler_params=_params("arbitrary", "arbitrary"),
        name="odd_proj",
    )(h, mod_l, cos, sin, *weights)


def _window_kernel(*refs, nb, local):
    if local:
        sink_ref, q_ref, kp_ref, kc_ref, kn_ref, vp_ref, vc_ref, vn_ref, kx_ref, vx_ref, o_ref = refs
    else:
        sink_ref, q_ref, kx_ref, vx_ref, o_ref = refs
    i = pl.program_id(1)
    tq = q_ref.shape[1]
    rows = GQA_REP * tq
    lo = _lane_lo((tq, LANES))
    if local:
        span = tq + 2 * WINDOW
        t = lax.broadcasted_iota(jnp.int32, (rows, span), 0) % tq
        j = lax.broadcasted_iota(jnp.int32, (rows, span), 1)
        rel = j - WINDOW - t
        valid = (jnp.abs(rel) <= WINDOW) & ((i > 0) | (j >= WINDOW)) & ((i < nb - 1) | (j < WINDOW + tq))
    for gp in range(GQA_HKV // 2):
        ksl = slice(gp * LANES, (gp + 1) * LANES)
        kx = kx_ref[0, :, ksl]
        vx = vx_ref[0, :, ksl]
        if local:
            kl = jnp.concatenate([kp_ref[0, :, ksl], kc_ref[0, :, ksl], kn_ref[0, :, ksl]], axis=0)
            vl = jnp.concatenate([vp_ref[0, :, ksl], vc_ref[0, :, ksl], vn_ref[0, :, ksl]], axis=0)
        halves = []
        for half in range(2):
            keep = lo if half == 0 else jnp.logical_not(lo)
            qs = jnp.concatenate(
                [jnp.where(keep, q_ref[0, :, (gp * GQA_REP + r) * LANES:(gp * GQA_REP + r + 1) * LANES], 0)
                 for r in range(GQA_REP)], axis=0)
            sink = jnp.concatenate(
                [jnp.full((tq, 1), sink_ref[(2 * gp + half) * GQA_REP + r], F32) for r in range(GQA_REP)], axis=0)
            s_x = _dot_nt(qs, kx)
            m = jnp.maximum(jnp.max(s_x, axis=-1, keepdims=True), sink)
            if local:
                s_l = jnp.where(valid, _dot_nt(qs, kl), -jnp.inf)
                m = jnp.maximum(m, jnp.max(s_l, axis=-1, keepdims=True))
            p_x = jnp.exp(s_x - m)
            den = jnp.sum(p_x, axis=-1, keepdims=True) + jnp.exp(sink - m)
            num = _dot(p_x.astype(BF16), vx)
            if local:
                p_l = jnp.exp(s_l - m)
                den = den + jnp.sum(p_l, axis=-1, keepdims=True)
                num = num + _dot(p_l.astype(BF16), vl)
            halves.append(num / den)
        for r in range(GQA_REP):
            col = gp * GQA_REP + r
            o_ref[0, :, col * LANES:(col + 1) * LANES] = jnp.where(
                lo, halves[0][r * tq:(r + 1) * tq], halves[1][r * tq:(r + 1) * tq]).astype(BF16)


def _window_attn(q, kx, vx, sink, local_kv=None):
    bsz, n, nq = q.shape
    nkv = kx.shape[2]
    tq = WINDOW
    nb = n // tq
    local = local_kv is not None
    qspec = pl.BlockSpec((1, tq, nq), lambda b, i: (b, i, 0))
    xspec = pl.BlockSpec((1, kx.shape[1], nkv), lambda b, i: (b, 0, 0))
    in_specs = [pl.BlockSpec(memory_space=pltpu.SMEM), qspec]
    args = [sink, q]
    if local:
        k, v = local_kv
        prev = pl.BlockSpec((1, tq, nkv), lambda b, i: (b, jnp.maximum(i - 1, 0), 0))
        cur = pl.BlockSpec((1, tq, nkv), lambda b, i: (b, i, 0))
        nxt = pl.BlockSpec((1, tq, nkv), lambda b, i: (b, jnp.minimum(i + 1, nb - 1), 0))
        in_specs += [prev, cur, nxt, prev, cur, nxt]
        args += [k, k, k, v, v, v]
    in_specs += [xspec, xspec]
    args += [kx, vx]
    return pl.pallas_call(
        functools.partial(_window_kernel, nb=nb, local=local),
        grid=(bsz, nb),
        in_specs=in_specs,
        out_specs=qspec,
        out_shape=jax.ShapeDtypeStruct(q.shape, BF16),
        compiler_params=_params("arbitrary", "arbitrary"),
        name="window_attn" if local else "sink_attn",
    )(*args)


_ROT_SRC = np.concatenate([np.arange(16, 32), np.arange(0, 16), np.arange(48, 64), np.arange(32, 48)])
_ROT_SIGN = np.concatenate([-np.ones(16), np.ones(16), -np.ones(16), np.ones(16)]).astype(np.float32)


def _rope_tables(n):
    quarter = MLA_DR // 4
    freqs = ROPE_BASE ** (-jnp.arange(quarter, dtype=F32) / quarter)
    t = jnp.arange(n)
    row = (t // GRID_W).astype(F32)
    col = (t % GRID_W).astype(F32)
    ang_r = row[:, None] * freqs
    ang_c = col[:, None] * freqs
    cos = jnp.concatenate([jnp.cos(ang_r), jnp.cos(ang_r), jnp.cos(ang_c), jnp.cos(ang_c)], axis=-1)
    sin = jnp.concatenate([jnp.sin(ang_r), jnp.sin(ang_r), jnp.sin(ang_c), jnp.sin(ang_c)], axis=-1)
    return cos, sin


def _ffn_weights(w_in, w_out):
    d = w_in.shape[0]
    wg = w_in[:, :FFN_DIM].reshape(d, N_FFN_CHUNKS, FFN_CHUNK).transpose(1, 0, 2).astype(BF16)
    wu = w_in[:, FFN_DIM:].reshape(d, N_FFN_CHUNKS, FFN_CHUNK).transpose(1, 0, 2).astype(BF16)
    return wg, wu, w_out.astype(BF16)


def _pad_lanes(row, offset=0):
    out = jnp.zeros((1, LANES), F32)
    return out.at[0, offset:offset + row.shape[0]].set(row.astype(F32))


def _even_weights(w_in, conv_w, conv_b, dt_bias, a_log, d_skip, ssd_norm, qa_norm, w_qb, kv_norm,
                  w_kvb, qn_g, qr_g, kn_g, kr_g, w_out):
    d = w_in.shape[0]
    sign = jnp.asarray(_ROT_SIGN)
    w_dt = w_in[:, _E1:_E2]
    w_kr = w_in[:, _E4:]
    wm = jnp.concatenate([w_dt, jnp.zeros((d, LANES - 2 * SSD_HEADS), F32), w_kr, w_kr[:, _ROT_SRC] * sign], axis=1)
    wq = w_qb.reshape(MLA_Q_RANK, MLA_HEADS, MLA_DN + MLA_DR)
    wq_r = wq[:, :, MLA_DN:]
    wqr = jnp.concatenate([wq_r, wq_r[:, :, _ROT_SRC] * sign], axis=-1).reshape(MLA_Q_RANK, MLA_HEADS * LANES)
    wkv = w_kvb.reshape(MLA_KV_RANK, MLA_HEADS, MLA_DN + MLA_DV)
    row = lambda g: g.reshape(1, -1).astype(F32)
    return {
        "wz": w_in[:, :_E0].astype(BF16),
        "wx": w_in[:, _E0:_E1].astype(BF16),
        "wm": wm.astype(BF16),
        "wqa": w_in[:, _E2:_E3].astype(BF16),
        "wkv": w_in[:, _E3:_E4].astype(BF16),
        "wqn": wq[:, :, :MLA_DN].reshape(MLA_Q_RANK, MLA_HEADS * MLA_DN).astype(BF16),
        "wqr": wqr.astype(BF16),
        "wkn": wkv[:, :, :MLA_DN].reshape(MLA_KV_RANK, MLA_HEADS * MLA_DN).astype(BF16),
        "wv": wkv[:, :, MLA_DN:].reshape(MLA_KV_RANK, MLA_HEADS * MLA_DV).astype(BF16),
        "gqa": row(qa_norm), "gkv": row(kv_norm), "gqn": row(qn_g), "gkn": row(kn_g),
        "gqr": row(jnp.concatenate([qr_g, qr_g[_ROT_SRC]])),
        "gkr": row(jnp.concatenate([kr_g, kr_g[_ROT_SRC]])),
        "conv_w": conv_w.astype(F32), "conv_b": row(conv_b),
        "dt_bias": _pad_lanes(dt_bias.reshape(-1)),
        "a_fwd": _pad_lanes(-jnp.exp(a_log[0].astype(F32)), 0),
        "a_rev": _pad_lanes(-jnp.exp(a_log[1].astype(F32)), SSD_HEADS),
        "d_skip": row(jnp.repeat(d_skip, SSD_P)), "ssd_norm": row(ssd_norm),
        "wo_ssd": w_out[:SSD_DI].astype(BF16), "wo_mla": w_out[SSD_DI:].astype(BF16),
    }


def _gqa_col_perm():
    perm = np.zeros(GQA_HQ * GQA_HD, np.int32)
    for gp in range(GQA_HKV // 2):
        for r in range(GQA_REP):
            for half in range(2):
                head = (2 * gp + half) * GQA_REP + r
                new = ((gp * GQA_REP + r) * 2 + half) * GQA_HD
                perm[new:new + GQA_HD] = np.arange(head * GQA_HD, (head + 1) * GQA_HD)
    return perm


_GQA_PERM = _gqa_col_perm()


def _odd_weights(w_in, q_g, k_g, sink, w_out):
    nq = GQA_HQ * GQA_HD
    nkv = GQA_HKV * GQA_HD
    tile2 = lambda g: jnp.concatenate([g, g]).reshape(1, LANES).astype(F32)
    return {
        "wq": w_in[:, :nq][:, _GQA_PERM].astype(BF16),
        "wk": w_in[:, nq:nq + nkv].astype(BF16),
        "wv": w_in[:, nq + nkv:].astype(BF16),
        "gq": tile2(q_g), "gk": tile2(k_g),
        "sink": sink.astype(F32),
        "wo": w_out[_GQA_PERM, :].astype(BF16),
    }


def kernel(x, c, ctx, c_ctx, w_mod, b_mod, w_ff1_in, w_ff1_out, w_ff2_in, w_ff2_out, w_in_e, conv_w, conv_b, dt_bias, a_log, d_skip, ssd_norm, mla_qa_norm, w_qb, mla_kv_norm, w_kvb, mla_qn_norm, mla_qr_norm, mla_kn_norm, mla_kr_norm, w_out_e, w_in_o, gqa_q_norm, gqa_k_norm, sink, w_out_o):
    bsz, n, d = x.shape
    lc = ctx.shape[1]
    depth = w_mod.shape[0]
    assert d == D_MODEL and bsz + 1 <= MOD_ROWS and n % TOKEN_TILE == 0 and lc % CHUNK == 0

    cond = jnp.zeros((MOD_ROWS, d), F32).at[:bsz].set(c).at[bsz].set(c_ctx)
    mod = _mod_vectors(cond, w_mod, b_mod).reshape(depth, MOD_ROWS, N_MOD, d)
    lat_row = lambda b: b
    ctx_row = lambda b: bsz

    cos64, sin64 = _rope_tables(n)
    tab_lat = jnp.concatenate([cos64, sin64], axis=-1)
    tab_ctx = jnp.concatenate([jnp.ones((lc, MLA_DR), F32), jnp.zeros((lc, MLA_DR), F32)], axis=-1)
    sign = jnp.asarray(_ROT_SIGN)
    cos_lat = jnp.concatenate([cos64, cos64], axis=-1)
    sin_lat = jnp.concatenate([sin64 * sign, sin64 * sign], axis=-1)
    cos_ctx = jnp.ones((lc, LANES), F32)
    sin_ctx = jnp.zeros((lc, LANES), F32)

    h, hc = x, ctx
    for l in range(depth):
        need_ctx = l < depth - 1
        mod_l = mod[l]
        ff1 = _ffn_weights(w_ff1_in[l], w_ff1_out[l])
        ff2 = _ffn_weights(w_ff2_in[l], w_ff2_out[l])
        h = _ffn(h, mod_l, lat_row, *ff1, k0=0)
        hc = _ffn(hc, mod_l, ctx_row, *ff1, k0=0)
        if l % 2 == 0:
            e = l // 2
            w = _even_weights(w_in_e[e], conv_w[e], conv_b[e], dt_bias[e], a_log[e], d_skip[e], ssd_norm[e],
                              mla_qa_norm[e], w_qb[e], mla_kv_norm[e], w_kvb[e], mla_qn_norm[e],
                              mla_qr_norm[e], mla_kn_norm[e], mla_kr_norm[e], w_out_e[e])
            zc, xbcc, dtc, qc, kc, vc = _even_proj(hc, mod_l, ctx_row, tab_ctx, w)
            z, xbc, dtm, q, k, v = _even_proj(h, mod_l, lat_row, tab_lat, w)
            s0 = jnp.zeros((bsz, SSD_HEADS // 2, SSD_N, LANES), F32)
            yfc, s_fc = _ssd(xbcc, dtc, s0, w, reverse=False)
            ysc, s_bc = _ssd(xbcc, dtc, s0, w, reverse=True, merge=(yfc, zc))
            yf, _ = _ssd(xbc, dtm, s_fc, w, reverse=False)
            ys, _ = _ssd(xbc, dtm, s_bc, w, reverse=True, merge=(yf, z))
            o = _flash(q, k, v, ctx_kv=(kc, vc))
            h = _outproj(h, mod_l, lat_row, [ys, o], [w["wo_ssd"], w["wo_mla"]])
            if need_ctx:
                oc = _flash(qc, kc, vc)
                hc = _outproj(hc, mod_l, ctx_row, [ysc, oc], [w["wo_ssd"], w["wo_mla"]])
        else:
            o_ = l // 2
            w = _odd_weights(w_in_o[o_], gqa_q_norm[o_], gqa_k_norm[o_], sink[o_], w_out_o[o_])
            qc, kc, vc = _odd_proj(hc, mod_l, ctx_row, cos_ctx, sin_ctx, w)
            q, k, v = _odd_proj(h, mod_l, lat_row, cos_lat, sin_lat, w)
            o = _window_attn(q, kc, vc, w["sink"], local_kv=(k, v))
            h = _outproj(h, mod_l, lat_row, [o], [w["wo"]])
            if need_ctx:
                oc = _window_attn(qc, kc, vc, w["sink"])
                hc = _outproj(hc, mod_l, ctx_row, [oc], [w["wo"]])
        h = _ffn(h, mod_l, lat_row, *ff2, k0=6)
        if need_ctx:
            hc = _ffn(hc, mod_l, ctx_row, *ff2, k0=6)
    return h
```

```python
import functools
import math

import jax
import jax.numpy as jnp
import numpy as np
from jax import lax
from jax.experimental import pallas as pl
from jax.experimental.pallas import tpu as pltpu

F32 = jnp.float32
BF16 = jnp.bfloat16
HIGHEST = lax.Precision.HIGHEST

D_MODEL = 1024
GRID_W = 64
N_MOD = 9
FFN_DIM = 2816
SSD_P = 64
SSD_HEADS = 16
SSD_DI = 1024
SSD_G = 2
SSD_N = 128
D_CONV = 5
CONV_CH = SSD_DI + 2 * SSD_G * SSD_N
CHUNK = 128
MLA_DN = 128
MLA_DR = 64
MLA_DV = 128
MLA_HEADS = 8
MLA_Q_RANK = 384
MLA_KV_RANK = 256
GQA_HD = 64
GQA_HQ = 16
GQA_HKV = 4
GQA_REP = 4
WINDOW = 128
ROPE_BASE = 10000.0
EPS = 1e-6
_E0 = SSD_DI
_E1 = _E0 + CONV_CH
_E2 = _E1 + 2 * SSD_HEADS
_E3 = _E2 + MLA_Q_RANK
_E4 = _E3 + MLA_KV_RANK
E_IN = _E4 + MLA_DR
MLA_SCALE = (MLA_DN + MLA_DR) ** -0.5
GQA_SCALE = GQA_HD ** -0.5
LOG2E = math.log2(math.e)
Q_PRESCALE = MLA_SCALE * LOG2E
GQA_PRESCALE = GQA_SCALE * LOG2E

LANES = 128
SUBLANES = 8
BF16_ROWS = 16
VMEM_LIMIT_BYTES = 56 * 1024 * 1024
MOD_ROWS = 16
FFN_CHUNK = 256
N_FFN_CHUNKS = FFN_DIM // FFN_CHUNK
TOKEN_TILE = 512
FLASH_TQ = 512
FLASH_LAG = 2
FLASH_GROUP = 6
HALO = SUBLANES


def _params(*sem):
    return pltpu.CompilerParams(dimension_semantics=sem, vmem_limit_bytes=VMEM_LIMIT_BYTES)


def _const_spec(shape):
    nd = len(shape)
    return pl.BlockSpec(shape, lambda *_: (0,) * nd)


def _dot(a, b):
    return jnp.dot(a, b, preferred_element_type=F32)


def _dot_nt(a, b):
    return lax.dot_general(a, b, (((1,), (1,)), ((), ())), preferred_element_type=F32)


def _silu(x):
    return x * jax.nn.sigmoid(x)


def _rms(x):
    return x * lax.rsqrt(jnp.mean(x * x, axis=-1, keepdims=True) + EPS)


def _modulate(h, shift, scale):
    return _rms(h) * (1.0 + scale) + shift


def _lane_lo(shape):
    return (lax.broadcasted_iota(jnp.int32, shape, len(shape) - 1) % LANES) < (LANES // 2)


def _mod_kernel(s_ref, w_ref, b_ref, o_ref):
    s = _silu(s_ref[...])
    o_ref[0] = jnp.dot(s, w_ref[0], preferred_element_type=F32, precision=HIGHEST) + b_ref[0]


def _mod_vectors(cond, w_mod, b_mod):
    depth, d, nm = w_mod.shape
    tn = 1024
    return pl.pallas_call(
        _mod_kernel,
        grid=(depth, nm // tn),
        in_specs=[
            pl.BlockSpec((MOD_ROWS, d), lambda l, j: (0, 0)),
            pl.BlockSpec((1, d, tn), lambda l, j: (l, 0, j)),
            pl.BlockSpec((1, 1, tn), lambda l, j: (l, 0, j)),
        ],
        out_specs=pl.BlockSpec((1, MOD_ROWS, tn), lambda l, j: (l, 0, j)),
        out_shape=jax.ShapeDtypeStruct((depth, MOD_ROWS, nm), F32),
        compiler_params=_params("arbitrary", "arbitrary"),
        name="mod_vectors",
    )(cond, w_mod, b_mod.reshape(depth, 1, nm))


def _ffn_kernel(h_ref, m_ref, wg_ref, wu_ref, wo_ref, o_ref, a_ref, *, k0):
    h = h_ref[0]
    xm = _modulate(h, m_ref[0, k0:k0 + 1, :], m_ref[0, k0 + 1:k0 + 2, :]).astype(BF16)
    for j in range(N_FFN_CHUNKS):
        g = _dot(xm, wg_ref[j])
        u = _dot(xm, wu_ref[j])
        a_ref[:, j * FFN_CHUNK:(j + 1) * FFN_CHUNK] = (_silu(g) * u).astype(BF16)
    y = _dot(a_ref[...], wo_ref[...])
    o_ref[0] = h + (0.5 * m_ref[0, k0 + 2:k0 + 3, :]) * y


def _token_tile(n):
    return min(TOKEN_TILE, n)


def _ffn(h, mod_l, row_of, wg, wu, wo, k0):
    bsz, n, d = h.shape
    tm = _token_tile(n)
    return pl.pallas_call(
        functools.partial(_ffn_kernel, k0=k0),
        grid=(bsz, n // tm),
        in_specs=[
            pl.BlockSpec((1, tm, d), lambda b, i: (b, i, 0)),
            pl.BlockSpec((1, N_MOD, d), lambda b, i: (row_of(b), 0, 0)),
            _const_spec(wg.shape),
            _const_spec(wu.shape),
            _const_spec(wo.shape),
        ],
        out_specs=pl.BlockSpec((1, tm, d), lambda b, i: (b, i, 0)),
        out_shape=jax.ShapeDtypeStruct(h.shape, F32),
        scratch_shapes=[pltpu.VMEM((tm, FFN_DIM), BF16)],
        compiler_params=_params("arbitrary", "arbitrary"),
        name="ffn",
    )(h, mod_l, wg, wu, wo)


def _outproj_kernel(*refs, n_in):
    h_ref, m_ref = refs[0], refs[1]
    a_refs = refs[2:2 + n_in]
    w_refs = refs[2 + n_in:2 + 2 * n_in]
    o_ref = refs[2 + 2 * n_in]
    acc = _dot(a_refs[0][0], w_refs[0][...])
    for a_ref, w_ref in zip(a_refs[1:], w_refs[1:]):
        acc = acc + _dot(a_ref[0], w_ref[...])
    o_ref[0] = h_ref[0] + m_ref[0, 5:6, :] * acc


def _outproj(h, mod_l, row_of, acts, weights):
    bsz, n, d = h.shape
    tm = _token_tile(n)
    n_in = len(acts)
    return pl.pallas_call(
        functools.partial(_outproj_kernel, n_in=n_in),
        grid=(bsz, n // tm),
        in_specs=[
            pl.BlockSpec((1, tm, d), lambda b, i: (b, i, 0)),
            pl.BlockSpec((1, N_MOD, d), lambda b, i: (row_of(b), 0, 0)),
        ] + [pl.BlockSpec((1, tm, a.shape[-1]), lambda b, i: (b, i, 0)) for a in acts]
        + [_const_spec(w.shape) for w in weights],
        out_specs=pl.BlockSpec((1, tm, d), lambda b, i: (b, i, 0)),
        out_shape=jax.ShapeDtypeStruct(h.shape, F32),
        compiler_params=_params("arbitrary", "arbitrary"),
        name="outproj",
    )(h, mod_l, *acts, *weights)


def _norm_rope_pair(t2, g2, tab):
    lo = _lane_lo(t2.shape)
    ss = jnp.sum(jnp.where(lo, t2 * t2, 0.0), axis=-1, keepdims=True) * (1.0 / MLA_DR)
    y = t2 * lax.rsqrt(ss + EPS) * g2 * tab
    return y + pltpu.roll(y, LANES // 2, 1)


def _even_proj_kernel(h_ref, m_ref, tab_ref, wz_ref, wx_ref, wm_ref, wqa_ref, wkv_ref, wqn_ref,
                      wqr_ref, wkn_ref, wv_ref, gqa_ref, gkv_ref, gqn_ref, gqr_ref, gkn_ref,
                      gkr_ref, z_ref, xbc_ref, dt_ref, qt_ref, k_ref, vt_ref):
    u = _modulate(h_ref[0], m_ref[0, 3:4, :], m_ref[0, 4:5, :]).astype(BF16)
    tab = tab_ref[...]
    z_ref[0] = _dot(u, wz_ref[...])
    xbc_ref[0] = _dot(u, wx_ref[...])
    misc = _dot(u, wm_ref[...])
    dt_ref[0] = misc[:, :LANES]
    kr = _norm_rope_pair(misc[:, LANES:], gkr_ref[...], tab)[:, :MLA_DR].astype(BF16)
    qa = (_rms(_dot(u, wqa_ref[...])) * gqa_ref[...]).astype(BF16)
    qn = _dot(qa, wqn_ref[...])
    qr2 = _dot(qa, wqr_ref[...])
    ckv = (_rms(_dot(u, wkv_ref[...])) * gkv_ref[...]).astype(BF16)
    kn = _dot(ckv, wkn_ref[...])
    v = _dot(ckv, wv_ref[...])
    for hd in range(MLA_HEADS):
        sl = slice(hd * LANES, (hd + 1) * LANES)
        qn_h = _rms(qn[:, sl]) * gqn_ref[...] * Q_PRESCALE
        qt_ref[0, hd, 0:MLA_DN, :] = qn_h.T.astype(BF16)
        qr = _norm_rope_pair(qr2[:, sl], gqr_ref[...], tab) * Q_PRESCALE
        qt_ref[0, hd, MLA_DN:MLA_DN + MLA_DR, :] = qr.T[:MLA_DR].astype(BF16)
        k_ref[0, hd, :, 0:MLA_DN] = (_rms(kn[:, sl]) * gkn_ref[...]).astype(BF16)
        k_ref[0, hd, :, MLA_DN:MLA_DN + MLA_DR] = kr
        vt_ref[0, hd, 0] = v[:, sl].T.astype(BF16)


def _even_proj(h, mod_l, row_of, tab, w):
    bsz, n, d = h.shape
    tm = _token_tile(n)
    weights = [w["wz"], w["wx"], w["wm"], w["wqa"], w["wkv"], w["wqn"], w["wqr"], w["wkn"], w["wv"],
               w["gqa"], w["gkv"], w["gqn"], w["gqr"], w["gkn"], w["gkr"]]
    dqk = MLA_DN + MLA_DR
    tok = lambda width: pl.BlockSpec((1, tm, width), lambda b, i: (b, i, 0))
    return pl.pallas_call(
        _even_proj_kernel,
        grid=(bsz, n // tm),
        in_specs=[
            tok(d),
            pl.BlockSpec((1, N_MOD, d), lambda b, i: (row_of(b), 0, 0)),
            pl.BlockSpec((tm, LANES), lambda b, i: (i, 0)),
        ] + [_const_spec(x.shape) for x in weights],
        out_specs=[
            tok(SSD_DI), tok(CONV_CH), tok(LANES),
            pl.BlockSpec((1, MLA_HEADS, dqk, tm), lambda b, i: (b, 0, 0, i)),
            pl.BlockSpec((1, MLA_HEADS, tm, dqk), lambda b, i: (b, 0, i, 0)),
            pl.BlockSpec((1, MLA_HEADS, 1, MLA_DV, tm), lambda b, i: (b, 0, i, 0, 0)),
        ],
        out_shape=[
            jax.ShapeDtypeStruct((bsz, n, SSD_DI), F32),
            jax.ShapeDtypeStruct((bsz, n, CONV_CH), F32),
            jax.ShapeDtypeStruct((bsz, n, LANES), F32),
            jax.ShapeDtypeStruct((bsz, MLA_HEADS, dqk, n), BF16),
            jax.ShapeDtypeStruct((bsz, MLA_HEADS, n, dqk), BF16),
            jax.ShapeDtypeStruct((bsz, MLA_HEADS, n // tm, MLA_DV, tm), BF16),
        ],
        compiler_params=_params("arbitrary", "arbitrary"),
        name="even_proj",
    )(h, mod_l, tab, *weights)


def _ssd_kernel(*refs, nc, reverse, lane0):
    if reverse:
        (xm_ref, xp_ref, xn_ref, dt_ref, cw_ref, cb_ref, dtb_ref, a_ref, h0_ref, yf_ref, z_ref,
         dsk_ref, nrm_ref, y_ref, st_ref, ext_ref, gat_ref) = refs
    else:
        (xm_ref, xp_ref, xn_ref, dt_ref, cw_ref, cb_ref, dtb_ref, a_ref, h0_ref,
         y_ref, st_ref, ext_ref) = refs
    c = pl.program_id(1)
    chunk = (nc - 1 - c) if reverse else c

    @pl.when(c == 0)
    def _():
        st_ref[...] = h0_ref[...]

    ext_ref[0:HALO, :] = jnp.where(chunk == 0, 0.0, xp_ref[0])
    ext_ref[HALO:HALO + CHUNK, :] = xm_ref[0]
    ext_ref[HALO + CHUNK:2 * HALO + CHUNK, :] = jnp.where(chunk == nc - 1, 0.0, xn_ref[0])
    acc = cb_ref[...]
    for k in range(D_CONV):
        off = HALO - D_CONV // 2 + k
        acc = acc + cw_ref[k:k + 1, :] * ext_ref[off:off + CHUNK, :]
    xbc = _silu(acc)
    xs = xbc[:, :SSD_DI]

    xr = dt_ref[0] + dtb_ref[...]
    dt = jnp.maximum(xr, 0.0) + jnp.log1p(jnp.exp(-jnp.abs(xr)))
    dta = dt * a_ref[...]
    ri = lax.broadcasted_iota(jnp.int32, (CHUNK, CHUNK), 0)
    ci = lax.broadcasted_iota(jnp.int32, (CHUNK, CHUNK), 1)
    tri = (ci >= ri) if reverse else (ri >= ci)
    acs = jnp.dot(tri.astype(F32), dta, preferred_element_type=F32, precision=HIGHEST)
    acs_t = acs.T
    dt_t = dt.T
    edge = 0 if reverse else CHUNK - 1
    lo = _lane_lo((CHUNK, LANES))
    lo_row = _lane_lo((1, LANES))

    cb_g, bt_g, c_g = [], [], []
    for g in range(SSD_G):
        bg = xbc[:, SSD_DI + g * SSD_N:SSD_DI + (g + 1) * SSD_N]
        cg = xbc[:, SSD_DI + SSD_G * SSD_N + g * SSD_N:SSD_DI + SSD_G * SSD_N + (g + 1) * SSD_N]
        cgb = cg.astype(BF16)
        cb_g.append(_dot_nt(cgb, bg.astype(BF16)))
        bt_g.append(bg.T)
        c_g.append(cgb)

    def head_terms(j, g):
        col = acs[:, j:j + 1]
        row = acs_t[j:j + 1, :]
        dtrow = dt_t[j:j + 1, :]
        lmat = jnp.exp(jnp.where(tri, col - row, -jnp.inf))
        gmat = (cb_g[g] * lmat * dtrow).astype(BF16)
        tot = acs_t[j:j + 1, edge:edge + 1]
        bw = (bt_g[g] * (jnp.exp(tot - row) * dtrow)).astype(BF16)
        return gmat, bw, jnp.exp(col), jnp.exp(tot)

    for k in range(SSD_HEADS // 2):
        g = (2 * k) // (SSD_HEADS // SSD_G)
        xpair = xs[:, k * LANES:(k + 1) * LANES]
        rhs = jnp.concatenate([jnp.where(lo, xpair, 0.0), jnp.where(lo, 0.0, xpair)], axis=0).astype(BF16)
        ga, bwa, ea, da = head_terms(lane0 + 2 * k, g)
        gb, bwb, eb, db = head_terms(lane0 + 2 * k + 1, g)
        s_in = st_ref[0, k]
        y = (_dot(jnp.concatenate([ga, gb], axis=1), rhs)
             + _dot(c_g[g], s_in.astype(BF16)) * jnp.where(lo, ea, eb))
        st_ref[0, k] = s_in * jnp.where(lo_row, da, db) + _dot(jnp.concatenate([bwa, bwb], axis=1), rhs)
        sl = slice(k * LANES, (k + 1) * LANES)
        if reverse:
            zt = z_ref[0, :, sl]
            gat_ref[:, sl] = (yf_ref[0, :, sl] + y + dsk_ref[:, sl] * xpair) * _silu(zt)
        else:
            y_ref[0, :, sl] = y

    if reverse:
        gw = SSD_DI // SSD_G
        for g in range(SSD_G):
            seg = gat_ref[:, g * gw:(g + 1) * gw]
            y_ref[0, :, g * gw:(g + 1) * gw] = (_rms(seg) * nrm_ref[:, g * gw:(g + 1) * gw]).astype(BF16)


def _ssd(xbc, dtm, h0, w, reverse, merge=None):
    bsz, n, _ = xbc.shape
    nc = n // CHUNK
    per = CHUNK // HALO
    nhb = n // HALO
    cidx = (lambda c: nc - 1 - c) if reverse else (lambda c: c)
    tok = lambda width: pl.BlockSpec((1, CHUNK, width), lambda b, c: (b, cidx(c), 0))
    state_spec = pl.BlockSpec((1, SSD_HEADS // 2, SSD_N, LANES), lambda b, c: (b, 0, 0, 0))
    in_specs = [
        tok(CONV_CH),
        pl.BlockSpec((1, HALO, CONV_CH), lambda b, c: (b, jnp.maximum(cidx(c) * per - 1, 0), 0)),
        pl.BlockSpec((1, HALO, CONV_CH), lambda b, c: (b, jnp.minimum((cidx(c) + 1) * per, nhb - 1), 0)),
        tok(LANES),
        _const_spec(w["conv_w"].shape),
        _const_spec(w["conv_b"].shape),
        _const_spec(w["dt_bias"].shape),
        _const_spec(w["a_rev" if reverse else "a_fwd"].shape),
        state_spec,
    ]
    args = [xbc, xbc, xbc, dtm, w["conv_w"], w["conv_b"], w["dt_bias"], w["a_rev" if reverse else "a_fwd"], h0]
    scratch = [pltpu.VMEM((CHUNK + 2 * HALO, CONV_CH), F32)]
    if reverse:
        y_fwd, z = merge
        in_specs += [tok(SSD_DI), tok(SSD_DI), _const_spec(w["d_skip"].shape), _const_spec(w["ssd_norm"].shape)]
        args += [y_fwd, z, w["d_skip"], w["ssd_norm"]]
        scratch.append(pltpu.VMEM((CHUNK, SSD_DI), F32))
    y_dtype = BF16 if reverse else F32
    return pl.pallas_call(
        functools.partial(_ssd_kernel, nc=nc, reverse=reverse, lane0=SSD_HEADS if reverse else 0),
        grid=(bsz, nc),
        in_specs=in_specs,
        out_specs=[tok(SSD_DI), state_spec],
        out_shape=[jax.ShapeDtypeStruct((bsz, n, SSD_DI), y_dtype), jax.ShapeDtypeStruct(h0.shape, F32)],
        scratch_shapes=scratch,
        compiler_params=_params("arbitrary", "arbitrary"),
        name="ssd_rev" if reverse else "ssd_fwd",
    )(*args)


def _flash_kernel(*refs, n_steps, has_ctx):
    if has_ctx:
        (qt_ref, k_ref, vt_ref, kc_ref, vct_ref, o_ref,
         s_ref, p_ref, al_ref, mx_ref, m_ref, acc_ref) = refs
    else:
        qt_ref, k_ref, vt_ref, o_ref, s_ref, p_ref, al_ref, mx_ref, m_ref, acc_ref = refs
    tk = vt_ref.shape[-1]
    dv = vt_ref.shape[-2]
    total = n_steps + (1 if has_ctx else 0)
    qt = qt_ref[0, 0]
    m_ref[...] = jnp.full(m_ref.shape, -jnp.inf, F32)
    acc_ref[...] = jnp.zeros(acc_ref.shape, F32)

    def is_ctx(step):
        return has_ctx and isinstance(step, int) and step == n_steps

    def rows_of(step):
        return kc_ref.shape[2] if is_ctx(step) else tk

    def scores(step, slot):
        if is_ctx(step):
            s = _dot(kc_ref[0, 0], qt)
        else:
            start = step * tk if isinstance(step, int) else pl.multiple_of(step * tk, tk)
            s = _dot(k_ref[0, 0, pl.ds(start, tk), :], qt)
        s_ref[slot, 0:rows_of(step)] = s
        mx_ref[slot] = jnp.max(s, axis=0, keepdims=True)

    def softmax(step, slot):
        rows = rows_of(step)
        s = s_ref[slot, 0:rows]
        m_old = m_ref[...]
        m_new = jnp.maximum(m_old, mx_ref[slot])
        m_ref[...] = m_new
        al_ref[slot] = jnp.exp2(m_old - m_new)
        p_ref[slot, 0:rows] = jnp.exp2(s - m_new).astype(BF16)

    def values(step, slot):
        vt = vct_ref[0, 0, 0] if is_ctx(step) else vt_ref[0, 0, step]
        vt1 = jnp.concatenate([vt, jnp.ones((BF16_ROWS, vt.shape[1]), BF16)], axis=0)
        acc_ref[...] = al_ref[slot] * acc_ref[...] + _dot(vt1, p_ref[slot, 0:rows_of(step)])

    def tick(t, parity):
        static = isinstance(t, int)
        if not static or 0 <= t - 2 * FLASH_LAG < total:
            values(t - 2 * FLASH_LAG, parity)
        if not static or 0 <= t - FLASH_LAG < total:
            softmax(t - FLASH_LAG, parity)
        if not static or t < total:
            scores(t, parity)

    first = 2 * FLASH_LAG
    n_groups = max(n_steps - first, 0) // FLASH_GROUP
    for t in range(first):
        tick(t, t % 2)
    if n_groups > 0:
        def group(i, carry):
            for u in range(FLASH_GROUP):
                tick(first + FLASH_GROUP * i + u, u % 2)
            return carry

        lax.fori_loop(0, n_groups, group, 0)
    for t in range(first + FLASH_GROUP * n_groups, total + 2 * FLASH_LAG):
        tick(t, t % 2)
    o_ref[0] = (acc_ref[0:dv, :] / acc_ref[dv:dv + 1, :]).T.astype(BF16)


def _flash(qt, k, vt, ctx_kv=None):
    bsz, nh, dqk, n = qt.shape
    n_steps, dv, tk = vt.shape[2:]
    tq = min(FLASH_TQ, n)
    has_ctx = ctx_kv is not None
    whole = lambda x: pl.BlockSpec((1, 1) + x.shape[2:], lambda b, h, i: (b, h) + (0,) * (x.ndim - 2))
    in_specs = [pl.BlockSpec((1, 1, dqk, tq), lambda b, h, i: (b, h, 0, i)), whole(k), whole(vt)]
    args = [qt, k, vt]
    if has_ctx:
        kc, vct = ctx_kv
        assert vct.shape[2] == 1 and kc.shape[2] <= tk
        in_specs += [whole(kc), whole(vct)]
        args += [kc, vct]
    return pl.pallas_call(
        functools.partial(_flash_kernel, n_steps=n_steps, has_ctx=has_ctx),
        grid=(bsz, nh, n // tq),
        in_specs=in_specs,
        out_specs=pl.BlockSpec((1, tq, dv), lambda b, h, i: (b, i, h)),
        out_shape=jax.ShapeDtypeStruct((bsz, n, nh * dv), BF16),
        scratch_shapes=[
            pltpu.VMEM((2, tk, tq), F32), pltpu.VMEM((2, tk, tq), BF16),
            pltpu.VMEM((2, 1, tq), F32), pltpu.VMEM((2, 1, tq), F32),
            pltpu.VMEM((1, tq), F32), pltpu.VMEM((dv + BF16_ROWS, tq), F32),
        ],
        compiler_params=_params("arbitrary", "arbitrary", "arbitrary"),
        name="mla_flash",
    )(*args)


def _odd_proj_kernel(h_ref, m_ref, cos_ref, sin_ref, wq_ref, wk_ref, wv_ref, gq_ref, gk_ref,
                     qt_ref, k_ref, vt_ref):
    u = _modulate(h_ref[0], m_ref[0, 3:4, :], m_ref[0, 4:5, :]).astype(BF16)
    cos = cos_ref[...]
    sin = sin_ref[...]
    lo = _lane_lo(cos.shape)
    first_quarter = (lax.broadcasted_iota(jnp.int32, cos.shape, 1) % (GQA_HD // 2)) < (GQA_HD // 4)

    def norm_rope(x2, g2):
        sq = x2 * x2
        s_lo = jnp.sum(jnp.where(lo, sq, 0.0), axis=-1, keepdims=True) * (1.0 / GQA_HD)
        s_hi = jnp.sum(jnp.where(lo, 0.0, sq), axis=-1, keepdims=True) * (1.0 / GQA_HD)
        y = x2 * jnp.where(lo, lax.rsqrt(s_lo + EPS), lax.rsqrt(s_hi + EPS)) * g2
        rot = jnp.where(first_quarter, pltpu.roll(y, LANES - GQA_HD // 4, 1), pltpu.roll(y, GQA_HD // 4, 1))
        return y * cos + rot * sin

    q = _dot(u, wq_ref[...])
    k = _dot(u, wk_ref[...])
    v = _dot(u, wv_ref[...])
    for j in range(q.shape[1] // LANES):
        sl = slice(j * LANES, (j + 1) * LANES)
        qt_ref[0, sl, :] = (norm_rope(q[:, sl], gq_ref[...]) * GQA_PRESCALE).T.astype(BF16)
    for j in range(k.shape[1] // LANES):
        sl = slice(j * LANES, (j + 1) * LANES)
        k_ref[0, :, sl] = norm_rope(k[:, sl], gk_ref[...]).astype(BF16)
        vt_ref[0, sl, :] = v[:, sl].T.astype(BF16)


def _odd_proj(h, mod_l, row_of, cos, sin, w):
    bsz, n, d = h.shape
    tm = _token_tile(n)
    weights = [w["wq"], w["wk"], w["wv"], w["gq"], w["gk"]]
    tok = lambda width: pl.BlockSpec((1, tm, width), lambda b, i: (b, i, 0))
    nq = GQA_HQ * GQA_HD
    nkv = GQA_HKV * GQA_HD
    return pl.pallas_call(
        _odd_proj_kernel,
        grid=(bsz, n // tm),
        in_specs=[
            tok(d),
            pl.BlockSpec((1, N_MOD, d), lambda b, i: (row_of(b), 0, 0)),
            pl.BlockSpec((tm, LANES), lambda b, i: (i, 0)),
            pl.BlockSpec((tm, LANES), lambda b, i: (i, 0)),
        ] + [_const_spec(x.shape) for x in weights],
        out_specs=[
            pl.BlockSpec((1, nq, tm), lambda b, i: (b, 0, i)),
            tok(nkv),
            pl.BlockSpec((1, nkv, tm), lambda b, i: (b, 0, i)),
        ],
        out_shape=[
            jax.ShapeDtypeStruct((bsz, nq, n), BF16),
            jax.ShapeDtypeStruct((bsz, n, nkv), BF16),
            jax.ShapeDtypeStruct((bsz, nkv, n), BF16),
        ],
        compiler_params=_params("arbitrary", "arbitrary"),
        name="odd_proj",
    )(h, mod_l, cos, sin, *weights)


def _band_bias(tq):
    j = np.arange(tq + 2 * WINDOW)[:, None]
    t = (np.arange(GQA_REP * tq) % tq)[None, :]
    return np.where(np.abs(j - WINDOW - t) <= WINDOW, 0.0, -np.inf).astype(np.float32)


def _window_kernel(*refs, nb, local):
    if local:
        (sink_ref, qt_ref, kp_ref, kc_ref, kn_ref, vp_ref, vc_ref, vn_ref, kx_ref, vx_ref, bias_ref,
         o_ref) = refs
    else:
        sink_ref, qt_ref, kx_ref, vx_ref, o_ref = refs
    i = pl.program_id(1)
    tq = qt_ref.shape[2]
    cols = GQA_REP * tq
    top = lax.broadcasted_iota(jnp.int32, (LANES, cols), 0) < GQA_HD
    if local:
        span = tq + 2 * WINDOW
        ninf = jnp.float32(-jnp.inf)
        bias = jnp.concatenate([
            jnp.where(i > 0, bias_ref[0:WINDOW, :], ninf),
            bias_ref[WINDOW:WINDOW + tq, :],
            jnp.where(i < nb - 1, bias_ref[WINDOW + tq:span, :], ninf)], axis=0)
    for gp in range(GQA_HKV // 2):
        fsl = slice(gp * LANES, (gp + 1) * LANES)
        if local:
            k_all = jnp.concatenate(
                [kp_ref[0, :, fsl], kc_ref[0, :, fsl], kn_ref[0, :, fsl], kx_ref[0, :, fsl]], axis=0)
            vt_all = jnp.concatenate(
                [vp_ref[0, fsl, :], vc_ref[0, fsl, :], vn_ref[0, fsl, :], vx_ref[0, fsl, :]], axis=1)
        else:
            k_all = kx_ref[0, :, fsl]
            vt_all = vx_ref[0, fsl, :]
        vt1 = jnp.concatenate([vt_all, jnp.ones((BF16_ROWS, vt_all.shape[1]), BF16)], axis=0)
        qs = jnp.concatenate(
            [qt_ref[0, (gp * GQA_REP + r) * LANES:(gp * GQA_REP + r + 1) * LANES, :] for r in range(GQA_REP)],
            axis=1)
        halves = []
        for half in range(2):
            keep = top if half == 0 else jnp.logical_not(top)
            qh = jnp.where(keep, qs, jnp.zeros_like(qs))
            sink = jnp.concatenate(
                [jnp.full((1, tq), sink_ref[(2 * gp + half) * GQA_REP + r] * LOG2E, F32) for r in range(GQA_REP)],
                axis=1)
            s = _dot(k_all, qh)
            if local:
                parts = [s[0:span] + bias, s[span:]]
            else:
                parts = [s]
            m = sink
            for part in parts:
                m = jnp.maximum(m, jnp.max(part, axis=0, keepdims=True))
            p = jnp.concatenate([jnp.exp2(part - m).astype(BF16) for part in parts], axis=0)
            acc = _dot(vt1, p)
            halves.append(acc[0:LANES] / (acc[LANES:LANES + 1] + jnp.exp2(sink - m)))
        ot = jnp.where(top, halves[0], halves[1])
        for r in range(GQA_REP):
            col = gp * GQA_REP + r
            o_ref[0, :, col * LANES:(col + 1) * LANES] = ot[:, r * tq:(r + 1) * tq].T.astype(BF16)


def _window_attn(qt, kx, vxt, sink, local_kv=None):
    bsz, nq, n = qt.shape
    nkv = kx.shape[2]
    tq = WINDOW
    nb = n // tq
    local = local_kv is not None
    in_specs = [pl.BlockSpec(memory_space=pltpu.SMEM), pl.BlockSpec((1, nq, tq), lambda b, i: (b, 0, i))]
    args = [sink, qt]
    if local:
        k, vt = local_kv
        prev = lambda i: jnp.maximum(i - 1, 0)
        nxt = lambda i: jnp.minimum(i + 1, nb - 1)
        in_specs += [pl.BlockSpec((1, tq, nkv), lambda b, i: (b, prev(i), 0)),
                     pl.BlockSpec((1, tq, nkv), lambda b, i: (b, i, 0)),
                     pl.BlockSpec((1, tq, nkv), lambda b, i: (b, nxt(i), 0)),
                     pl.BlockSpec((1, nkv, tq), lambda b, i: (b, 0, prev(i))),
                     pl.BlockSpec((1, nkv, tq), lambda b, i: (b, 0, i)),
                     pl.BlockSpec((1, nkv, tq), lambda b, i: (b, 0, nxt(i)))]
        args += [k, k, k, vt, vt, vt]
    in_specs += [pl.BlockSpec((1,) + kx.shape[1:], lambda b, i: (b, 0, 0)),
                 pl.BlockSpec((1,) + vxt.shape[1:], lambda b, i: (b, 0, 0))]
    args += [kx, vxt]
    if local:
        bias = jnp.asarray(_band_bias(tq))
        in_specs.append(_const_spec(bias.shape))
        args.append(bias)
    return pl.pallas_call(
        functools.partial(_window_kernel, nb=nb, local=local),
        grid=(bsz, nb),
        in_specs=in_specs,
        out_specs=pl.BlockSpec((1, tq, nq), lambda b, i: (b, i, 0)),
        out_shape=jax.ShapeDtypeStruct((bsz, n, nq), BF16),
        compiler_params=_params("arbitrary", "arbitrary"),
        name="window_attn" if local else "sink_attn",
    )(*args)


_ROT_SRC = np.concatenate([np.arange(16, 32), np.arange(0, 16), np.arange(48, 64), np.arange(32, 48)])
_ROT_SIGN = np.concatenate([-np.ones(16), np.ones(16), -np.ones(16), np.ones(16)]).astype(np.float32)


def _rope_tables(n):
    quarter = MLA_DR // 4
    freqs = ROPE_BASE ** (-jnp.arange(quarter, dtype=F32) / quarter)
    t = jnp.arange(n)
    row = (t // GRID_W).astype(F32)
    col = (t % GRID_W).astype(F32)
    ang_r = row[:, None] * freqs
    ang_c = col[:, None] * freqs
    cos = jnp.concatenate([jnp.cos(ang_r), jnp.cos(ang_r), jnp.cos(ang_c), jnp.cos(ang_c)], axis=-1)
    sin = jnp.concatenate([jnp.sin(ang_r), jnp.sin(ang_r), jnp.sin(ang_c), jnp.sin(ang_c)], axis=-1)
    return cos, sin


def _ffn_weights(w_in, w_out):
    d = w_in.shape[0]
    wg = w_in[:, :FFN_DIM].reshape(d, N_FFN_CHUNKS, FFN_CHUNK).transpose(1, 0, 2).astype(BF16)
    wu = w_in[:, FFN_DIM:].reshape(d, N_FFN_CHUNKS, FFN_CHUNK).transpose(1, 0, 2).astype(BF16)
    return wg, wu, w_out.astype(BF16)


def _pad_lanes(row, offset=0):
    out = jnp.zeros((1, LANES), F32)
    return out.at[0, offset:offset + row.shape[0]].set(row.astype(F32))


def _even_weights(w_in, conv_w, conv_b, dt_bias, a_log, d_skip, ssd_norm, qa_norm, w_qb, kv_norm,
                  w_kvb, qn_g, qr_g, kn_g, kr_g, w_out):
    d = w_in.shape[0]
    sign = jnp.asarray(_ROT_SIGN)
    w_dt = w_in[:, _E1:_E2]
    w_kr = w_in[:, _E4:]
    wm = jnp.concatenate([w_dt, jnp.zeros((d, LANES - 2 * SSD_HEADS), F32), w_kr, w_kr[:, _ROT_SRC] * sign], axis=1)
    wq = w_qb.reshape(MLA_Q_RANK, MLA_HEADS, MLA_DN + MLA_DR)
    wq_r = wq[:, :, MLA_DN:]
    wqr = jnp.concatenate([wq_r, wq_r[:, :, _ROT_SRC] * sign], axis=-1).reshape(MLA_Q_RANK, MLA_HEADS * LANES)
    wkv = w_kvb.reshape(MLA_KV_RANK, MLA_HEADS, MLA_DN + MLA_DV)
    row = lambda g: g.reshape(1, -1).astype(F32)
    return {
        "wz": w_in[:, :_E0].astype(BF16),
        "wx": w_in[:, _E0:_E1].astype(BF16),
        "wm": wm.astype(BF16),
        "wqa": w_in[:, _E2:_E3].astype(BF16),
        "wkv": w_in[:, _E3:_E4].astype(BF16),
        "wqn": wq[:, :, :MLA_DN].reshape(MLA_Q_RANK, MLA_HEADS * MLA_DN).astype(BF16),
        "wqr": wqr.astype(BF16),
        "wkn": wkv[:, :, :MLA_DN].reshape(MLA_KV_RANK, MLA_HEADS * MLA_DN).astype(BF16),
        "wv": wkv[:, :, MLA_DN:].reshape(MLA_KV_RANK, MLA_HEADS * MLA_DV).astype(BF16),
        "gqa": row(qa_norm), "gkv": row(kv_norm), "gqn": row(qn_g), "gkn": row(kn_g),
        "gqr": row(jnp.concatenate([qr_g, qr_g[_ROT_SRC]])),
        "gkr": row(jnp.concatenate([kr_g, kr_g[_ROT_SRC]])),
        "conv_w": conv_w.astype(F32), "conv_b": row(conv_b),
        "dt_bias": _pad_lanes(dt_bias.reshape(-1)),
        "a_fwd": _pad_lanes(-jnp.exp(a_log[0].astype(F32)), 0),
        "a_rev": _pad_lanes(-jnp.exp(a_log[1].astype(F32)), SSD_HEADS),
        "d_skip": row(jnp.repeat(d_skip, SSD_P)), "ssd_norm": row(ssd_norm),
        "wo_ssd": w_out[:SSD_DI].astype(BF16), "wo_mla": w_out[SSD_DI:].astype(BF16),
    }


def _gqa_col_perm():
    perm = np.zeros(GQA_HQ * GQA_HD, np.int32)
    for gp in range(GQA_HKV // 2):
        for r in range(GQA_REP):
            for half in range(2):
                head = (2 * gp + half) * GQA_REP + r
                new = ((gp * GQA_REP + r) * 2 + half) * GQA_HD
                perm[new:new + GQA_HD] = np.arange(head * GQA_HD, (head + 1) * GQA_HD)
    return perm


_GQA_PERM = _gqa_col_perm()


def _odd_weights(w_in, q_g, k_g, sink, w_out):
    nq = GQA_HQ * GQA_HD
    nkv = GQA_HKV * GQA_HD
    tile2 = lambda g: jnp.concatenate([g, g]).reshape(1, LANES).astype(F32)
    return {
        "wq": w_in[:, :nq][:, _GQA_PERM].astype(BF16),
        "wk": w_in[:, nq:nq + nkv].astype(BF16),
        "wv": w_in[:, nq + nkv:].astype(BF16),
        "gq": tile2(q_g), "gk": tile2(k_g),
        "sink": sink.astype(F32),
        "wo": w_out[_GQA_PERM, :].astype(BF16),
    }


def kernel(x, c, ctx, c_ctx, w_mod, b_mod, w_ff1_in, w_ff1_out, w_ff2_in, w_ff2_out, w_in_e, conv_w, conv_b, dt_bias, a_log, d_skip, ssd_norm, mla_qa_norm, w_qb, mla_kv_norm, w_kvb, mla_qn_norm, mla_qr_norm, mla_kn_norm, mla_kr_norm, w_out_e, w_in_o, gqa_q_norm, gqa_k_norm, sink, w_out_o):
    bsz, n, d = x.shape
    lc = ctx.shape[1]
    depth = w_mod.shape[0]
    assert d == D_MODEL and bsz + 1 <= MOD_ROWS and n % TOKEN_TILE == 0 and lc % CHUNK == 0

    cond = jnp.zeros((MOD_ROWS, d), F32).at[:bsz].set(c).at[bsz].set(c_ctx)
    mod = _mod_vectors(cond, w_mod, b_mod).reshape(depth, MOD_ROWS, N_MOD, d)
    lat_row = lambda b: b
    ctx_row = lambda b: bsz

    cos64, sin64 = _rope_tables(n)
    tab_lat = jnp.concatenate([cos64, sin64], axis=-1)
    tab_ctx = jnp.concatenate([jnp.ones((lc, MLA_DR), F32), jnp.zeros((lc, MLA_DR), F32)], axis=-1)
    sign = jnp.asarray(_ROT_SIGN)
    cos_lat = jnp.concatenate([cos64, cos64], axis=-1)
    sin_lat = jnp.concatenate([sin64 * sign, sin64 * sign], axis=-1)
    cos_ctx = jnp.ones((lc, LANES), F32)
    sin_ctx = jnp.zeros((lc, LANES), F32)

    h, hc = x, ctx
    for l in range(depth):
        need_ctx = l < depth - 1
        mod_l = mod[l]
        ff1 = _ffn_weights(w_ff1_in[l], w_ff1_out[l])
        ff2 = _ffn_weights(w_ff2_in[l], w_ff2_out[l])
        h = _ffn(h, mod_l, lat_row, *ff1, k0=0)
        hc = _ffn(hc, mod_l, ctx_row, *ff1, k0=0)
        if l % 2 == 0:
            e = l // 2
            w = _even_weights(w_in_e[e], conv_w[e], conv_b[e], dt_bias[e], a_log[e], d_skip[e], ssd_norm[e],
                              mla_qa_norm[e], w_qb[e], mla_kv_norm[e], w_kvb[e], mla_qn_norm[e],
                              mla_qr_norm[e], mla_kn_norm[e], mla_kr_norm[e], w_out_e[e])
            zc, xbcc, dtc, qc, kc, vc = _even_proj(hc, mod_l, ctx_row, tab_ctx, w)
            z, xbc, dtm, q, k, v = _even_proj(h, mod_l, lat_row, tab_lat, w)
            s0 = jnp.zeros((bsz, SSD_HEADS // 2, SSD_N, LANES), F32)
            yfc, s_fc = _ssd(xbcc, dtc, s0, w, reverse=False)
            ysc, s_bc = _ssd(xbcc, dtc, s0, w, reverse=True, merge=(yfc, zc))
            yf, _ = _ssd(xbc, dtm, s_fc, w, reverse=False)
            ys, _ = _ssd(xbc, dtm, s_bc, w, reverse=True, merge=(yf, z))
            o = _flash(q, k, v, ctx_kv=(kc, vc))
            h = _outproj(h, mod_l, lat_row, [ys, o], [w["wo_ssd"], w["wo_mla"]])
            if need_ctx:
                oc = _flash(qc, kc, vc)
                hc = _outproj(hc, mod_l, ctx_row, [ysc, oc], [w["wo_ssd"], w["wo_mla"]])
        else:
            o_ = l // 2
            w = _odd_weights(w_in_o[o_], gqa_q_norm[o_], gqa_k_norm[o_], sink[o_], w_out_o[o_])
            qc, kc, vc = _odd_proj(hc, mod_l, ctx_row, cos_ctx, sin_ctx, w)
            q, k, v = _odd_proj(h, mod_l, lat_row, cos_lat, sin_lat, w)
            o = _window_attn(q, kc, vc, w["sink"], local_kv=(k, v))
            h = _outproj(h, mod_l, lat_row, [o], [w["wo"]])
            if need_ctx:
                oc = _window_attn(qc, kc, vc, w["sink"])
                hc = _outproj(hc, mod_l, ctx_row, [oc], [w["wo"]])
        h = _ffn(h, mod_l, lat_row, *ff2, k0=6)
        if need_ctx:
            hc = _ffn(hc, mod_l, ctx_row, *ff2, k0=6)
    return h
```

```python
import functools
import math

import jax
import jax.numpy as jnp
import numpy as np
from jax import lax
from jax.experimental import pallas as pl
from jax.experimental.pallas import tpu as pltpu

F32 = jnp.float32
BF16 = jnp.bfloat16
HIGHEST = lax.Precision.HIGHEST

D_MODEL = 1024
GRID_W = 64
N_MOD = 9
FFN_DIM = 2816
SSD_P = 64
SSD_HEADS = 16
SSD_DI = 1024
SSD_G = 2
SSD_N = 128
D_CONV = 5
CONV_CH = SSD_DI + 2 * SSD_G * SSD_N
CHUNK = 128
MLA_DN = 128
MLA_DR = 64
MLA_DV = 128
MLA_HEADS = 8
MLA_Q_RANK = 384
MLA_KV_RANK = 256
GQA_HD = 64
GQA_HQ = 16
GQA_HKV = 4
GQA_REP = 4
WINDOW = 128
ROPE_BASE = 10000.0
EPS = 1e-6
_E0 = SSD_DI
_E1 = _E0 + CONV_CH
_E2 = _E1 + 2 * SSD_HEADS
_E3 = _E2 + MLA_Q_RANK
_E4 = _E3 + MLA_KV_RANK
E_IN = _E4 + MLA_DR
MLA_SCALE = (MLA_DN + MLA_DR) ** -0.5
GQA_SCALE = GQA_HD ** -0.5
LOG2E = math.log2(math.e)
Q_PRESCALE = MLA_SCALE * LOG2E
GQA_PRESCALE = GQA_SCALE * LOG2E

LANES = 128
SUBLANES = 8
BF16_ROWS = 16
VMEM_LIMIT_BYTES = 56 * 1024 * 1024
MOD_ROWS = 16
FFN_CHUNK = 256
N_FFN_CHUNKS = FFN_DIM // FFN_CHUNK
TOKEN_TILE = 512
FLASH_TQ = 1024
FLASH_LAG = 2
FLASH_GROUP = 6
HALO = SUBLANES


def _params(*sem):
    return pltpu.CompilerParams(dimension_semantics=sem, vmem_limit_bytes=VMEM_LIMIT_BYTES)


def _const_spec(shape):
    nd = len(shape)
    return pl.BlockSpec(shape, lambda *_: (0,) * nd, pipeline_mode=pl.Buffered(1))


def _dot(a, b):
    return jnp.dot(a, b, preferred_element_type=F32)


def _dot_nt(a, b):
    return lax.dot_general(a, b, (((1,), (1,)), ((), ())), preferred_element_type=F32)


def _silu(x):
    return x * jax.nn.sigmoid(x)


def _rms(x):
    return x * lax.rsqrt(jnp.mean(x * x, axis=-1, keepdims=True) + EPS)


def _modulate(h, shift, scale):
    return _rms(h) * (1.0 + scale) + shift


def _lane_lo(shape):
    return (lax.broadcasted_iota(jnp.int32, shape, len(shape) - 1) % LANES) < (LANES // 2)


def _mod_kernel(s_ref, w_ref, b_ref, o_ref):
    s = _silu(s_ref[...])
    o_ref[0] = jnp.dot(s, w_ref[0], preferred_element_type=F32, precision=HIGHEST) + b_ref[0]


def _mod_vectors(cond, w_mod, b_mod):
    depth, d, nm = w_mod.shape
    tn = 1024
    return pl.pallas_call(
        _mod_kernel,
        grid=(depth, nm // tn),
        in_specs=[
            pl.BlockSpec((MOD_ROWS, d), lambda l, j: (0, 0)),
            pl.BlockSpec((1, d, tn), lambda l, j: (l, 0, j)),
            pl.BlockSpec((1, 1, tn), lambda l, j: (l, 0, j)),
        ],
        out_specs=pl.BlockSpec((1, MOD_ROWS, tn), lambda l, j: (l, 0, j)),
        out_shape=jax.ShapeDtypeStruct((depth, MOD_ROWS, nm), F32),
        compiler_params=_params("arbitrary", "arbitrary"),
        name="mod_vectors",
    )(cond, w_mod, b_mod.reshape(depth, 1, nm))


def _ffn_kernel(*refs, k0, n_mix):
    h_ref, m_ref = refs[0], refs[1]
    mix_a = refs[2:2 + n_mix]
    mix_w = refs[2 + n_mix:2 + 2 * n_mix]
    wg_ref, wu_ref, wo_ref, o_ref, a_ref = refs[2 + 2 * n_mix:]
    h = h_ref[0]
    if n_mix:
        y = _dot(mix_a[0][0], mix_w[0][...])
        for ar, wr in zip(mix_a[1:], mix_w[1:]):
            y = y + _dot(ar[0], wr[...])
        h = h + m_ref[0, 5:6, :] * y
    xm = _modulate(h, m_ref[0, k0:k0 + 1, :], m_ref[0, k0 + 1:k0 + 2, :]).astype(BF16)
    for j in range(N_FFN_CHUNKS):
        g = _dot(xm, wg_ref[j])
        u = _dot(xm, wu_ref[j])
        a_ref[:, j * FFN_CHUNK:(j + 1) * FFN_CHUNK] = (_silu(g) * u).astype(BF16)
    y = _dot(a_ref[...], wo_ref[...])
    o_ref[0] = h + (0.5 * m_ref[0, k0 + 2:k0 + 3, :]) * y


def _token_tile(n):
    return min(TOKEN_TILE, n)


def _ffn(h, mod_l, row_of, wg, wu, wo, k0, mix=()):
    bsz, n, d = h.shape
    tm = _token_tile(n)
    acts = [a for a, _ in mix]
    mix_w = [x for _, x in mix]
    return pl.pallas_call(
        functools.partial(_ffn_kernel, k0=k0, n_mix=len(mix)),
        grid=(bsz, n // tm),
        in_specs=[
            pl.BlockSpec((1, tm, d), lambda b, i: (b, i, 0)),
            pl.BlockSpec((1, N_MOD, d), lambda b, i: (row_of(b), 0, 0)),
        ] + [pl.BlockSpec((1, tm, a.shape[-1]), lambda b, i: (b, i, 0)) for a in acts]
        + [_const_spec(x.shape) for x in mix_w + [wg, wu, wo]],
        out_specs=pl.BlockSpec((1, tm, d), lambda b, i: (b, i, 0)),
        out_shape=jax.ShapeDtypeStruct(h.shape, F32),
        scratch_shapes=[pltpu.VMEM((tm, FFN_DIM), BF16)],
        compiler_params=_params("arbitrary", "arbitrary"),
        name="ffn_mix" if mix else "ffn",
    )(h, mod_l, *acts, *mix_w, wg, wu, wo)


def _norm_rope_pair(t2, g2, tab):
    lo = _lane_lo(t2.shape)
    ss = jnp.sum(jnp.where(lo, t2 * t2, 0.0), axis=-1, keepdims=True) * (1.0 / MLA_DR)
    y = t2 * lax.rsqrt(ss + EPS) * g2 * tab
    return y + pltpu.roll(y, LANES // 2, 1)


def _even_proj_kernel(h_ref, m_ref, tab_ref, wz_ref, wx_ref, wm_ref, wqa_ref, wkv_ref, wqn_ref,
                      wqr_ref, wkn_ref, wv_ref, gqa_ref, gkv_ref, gqn_ref, gqr_ref, gkn_ref,
                      gkr_ref, z_ref, xbc_ref, dt_ref, qt_ref, k_ref, vt_ref):
    u = _modulate(h_ref[0], m_ref[0, 3:4, :], m_ref[0, 4:5, :]).astype(BF16)
    tab = tab_ref[...]
    z_ref[0] = _dot(u, wz_ref[...])
    xbc_ref[0] = _dot(u, wx_ref[...])
    misc = _dot(u, wm_ref[...])
    dt_ref[0] = misc[:, :LANES]
    kr = _norm_rope_pair(misc[:, LANES:], gkr_ref[...], tab)[:, :MLA_DR].astype(BF16)
    qa = (_rms(_dot(u, wqa_ref[...])) * gqa_ref[...]).astype(BF16)
    qn = _dot(qa, wqn_ref[...])
    qr2 = _dot(qa, wqr_ref[...])
    ckv = (_rms(_dot(u, wkv_ref[...])) * gkv_ref[...]).astype(BF16)
    kn = _dot(ckv, wkn_ref[...])
    v = _dot(ckv, wv_ref[...])
    for hd in range(MLA_HEADS):
        sl = slice(hd * LANES, (hd + 1) * LANES)
        qn_h = _rms(qn[:, sl]) * gqn_ref[...] * Q_PRESCALE
        qt_ref[0, hd, 0:MLA_DN, :] = qn_h.T.astype(BF16)
        qr = _norm_rope_pair(qr2[:, sl], gqr_ref[...], tab) * Q_PRESCALE
        qt_ref[0, hd, MLA_DN:MLA_DN + MLA_DR, :] = qr.T[:MLA_DR].astype(BF16)
        k_ref[0, hd, :, 0:MLA_DN] = (_rms(kn[:, sl]) * gkn_ref[...]).astype(BF16)
        k_ref[0, hd, :, MLA_DN:MLA_DN + MLA_DR] = kr
        vt_ref[0, hd, 0] = v[:, sl].T.astype(BF16)


def _even_proj(h, mod_l, row_of, tab, w):
    bsz, n, d = h.shape
    tm = _token_tile(n)
    weights = [w["wz"], w["wx"], w["wm"], w["wqa"], w["wkv"], w["wqn"], w["wqr"], w["wkn"], w["wv"],
               w["gqa"], w["gkv"], w["gqn"], w["gqr"], w["gkn"], w["gkr"]]
    dqk = MLA_DN + MLA_DR
    tok = lambda width: pl.BlockSpec((1, tm, width), lambda b, i: (b, i, 0))
    return pl.pallas_call(
        _even_proj_kernel,
        grid=(bsz, n // tm),
        in_specs=[
            tok(d),
            pl.BlockSpec((1, N_MOD, d), lambda b, i: (row_of(b), 0, 0)),
            pl.BlockSpec((tm, LANES), lambda b, i: (i, 0)),
        ] + [_const_spec(x.shape) for x in weights],
        out_specs=[
            tok(SSD_DI), tok(CONV_CH), tok(LANES),
            pl.BlockSpec((1, MLA_HEADS, dqk, tm), lambda b, i: (b, 0, 0, i)),
            pl.BlockSpec((1, MLA_HEADS, tm, dqk), lambda b, i: (b, 0, i, 0)),
            pl.BlockSpec((1, MLA_HEADS, 1, MLA_DV, tm), lambda b, i: (b, 0, i, 0, 0)),
        ],
        out_shape=[
            jax.ShapeDtypeStruct((bsz, n, SSD_DI), F32),
            jax.ShapeDtypeStruct((bsz, n, CONV_CH), F32),
            jax.ShapeDtypeStruct((bsz, n, LANES), F32),
            jax.ShapeDtypeStruct((bsz, MLA_HEADS, dqk, n), BF16),
            jax.ShapeDtypeStruct((bsz, MLA_HEADS, n, dqk), BF16),
            jax.ShapeDtypeStruct((bsz, MLA_HEADS, n // tm, MLA_DV, tm), BF16),
        ],
        compiler_params=_params("arbitrary", "arbitrary"),
        name="even_proj",
    )(h, mod_l, tab, *weights)


def _ssd_kernel(*refs, nc, reverse, lane0):
    if reverse:
        (xa_ref, dt_ref, dtb_ref, a_ref, h0_ref, yf_ref, z_ref, dsk_ref, nrm_ref,
         y_ref, st_ref, gat_ref) = refs
    else:
        (xm_ref, xp_ref, xn_ref, dt_ref, cw_ref, cb_ref, dtb_ref, a_ref, h0_ref,
         y_ref, st_ref, xa_ref, ext_ref) = refs
    c = pl.program_id(1)

    @pl.when(c == 0)
    def _():
        st_ref[...] = h0_ref[...]

    if reverse:
        xbc = xa_ref[0]
    else:
        ext_ref[0:HALO, :] = jnp.where(c == 0, 0.0, xp_ref[0])
        ext_ref[HALO:HALO + CHUNK, :] = xm_ref[0]
        ext_ref[HALO + CHUNK:2 * HALO + CHUNK, :] = jnp.where(c == nc - 1, 0.0, xn_ref[0])
        acc = cb_ref[...]
        for k in range(D_CONV):
            off = HALO - D_CONV // 2 + k
            acc = acc + cw_ref[k:k + 1, :] * ext_ref[off:off + CHUNK, :]
        xbc = _silu(acc)
        xa_ref[0] = xbc
    xs = xbc[:, :SSD_DI]

    xr = dt_ref[0] + dtb_ref[...]
    dt = jnp.maximum(xr, 0.0) + jnp.log1p(jnp.exp(-jnp.abs(xr)))
    dta = dt * a_ref[...]
    ri = lax.broadcasted_iota(jnp.int32, (CHUNK, CHUNK), 0)
    ci = lax.broadcasted_iota(jnp.int32, (CHUNK, CHUNK), 1)
    tri = (ci >= ri) if reverse else (ri >= ci)
    acs = jnp.dot(tri.astype(F32), dta, preferred_element_type=F32, precision=HIGHEST)
    acs_t = acs.T
    dt_t = dt.T
    edge = 0 if reverse else CHUNK - 1
    lo = _lane_lo((CHUNK, LANES))
    lo_row = _lane_lo((1, LANES))

    cb_g, bt_g, c_g = [], [], []
    for g in range(SSD_G):
        bg = xbc[:, SSD_DI + g * SSD_N:SSD_DI + (g + 1) * SSD_N]
        cg = xbc[:, SSD_DI + SSD_G * SSD_N + g * SSD_N:SSD_DI + SSD_G * SSD_N + (g + 1) * SSD_N]
        cgb = cg.astype(BF16)
        cb_g.append(_dot_nt(cgb, bg.astype(BF16)))
        bt_g.append(bg.T)
        c_g.append(cgb)

    def head_terms(j, g):
        col = acs[:, j:j + 1]
        row = acs_t[j:j + 1, :]
        dtrow = dt_t[j:j + 1, :]
        lmat = jnp.exp(jnp.where(tri, col - row, -jnp.inf))
        gmat = (cb_g[g] * lmat * dtrow).astype(BF16)
        tot = acs_t[j:j + 1, edge:edge + 1]
        bw = (bt_g[g] * (jnp.exp(tot - row) * dtrow)).astype(BF16)
        return gmat, bw, jnp.exp(col), jnp.exp(tot)

    for k in range(SSD_HEADS // 2):
        g = (2 * k) // (SSD_HEADS // SSD_G)
        xpair = xs[:, k * LANES:(k + 1) * LANES]
        rhs = jnp.concatenate([jnp.where(lo, xpair, 0.0), jnp.where(lo, 0.0, xpair)], axis=0).astype(BF16)
        ga, bwa, ea, da = head_terms(lane0 + 2 * k, g)
        gb, bwb, eb, db = head_terms(lane0 + 2 * k + 1, g)
        s_in = st_ref[0, k]
        y = (_dot(jnp.concatenate([ga, gb], axis=1), rhs)
             + _dot(c_g[g], s_in.astype(BF16)) * jnp.where(lo, ea, eb))
        st_ref[0, k] = s_in * jnp.where(lo_row, da, db) + _dot(jnp.concatenate([bwa, bwb], axis=1), rhs)
        sl = slice(k * LANES, (k + 1) * LANES)
        if reverse:
            zt = z_ref[0, :, sl]
            gat_ref[:, sl] = (yf_ref[0, :, sl] + y + dsk_ref[:, sl] * xpair) * _silu(zt)
        else:
            y_ref[0, :, sl] = y

    if reverse:
        gw = SSD_DI // SSD_G
        for g in range(SSD_G):
            seg = gat_ref[:, g * gw:(g + 1) * gw]
            y_ref[0, :, g * gw:(g + 1) * gw] = (_rms(seg) * nrm_ref[:, g * gw:(g + 1) * gw]).astype(BF16)


def _ssd(xbc, dtm, h0, w, reverse, merge=None):
    bsz, n, _ = xbc.shape
    nc = n // CHUNK
    per = CHUNK // HALO
    nhb = n // HALO
    cidx = (lambda c: nc - 1 - c) if reverse else (lambda c: c)
    tok = lambda width: pl.BlockSpec((1, CHUNK, width), lambda b, c: (b, cidx(c), 0))
    state_spec = pl.BlockSpec((1, SSD_HEADS // 2, SSD_N, LANES), lambda b, c: (b, 0, 0, 0))
    a_row = w["a_rev" if reverse else "a_fwd"]
    y_spec = [tok(SSD_DI), state_spec]
    y_shape = [jax.ShapeDtypeStruct((bsz, n, SSD_DI), BF16 if reverse else F32), jax.ShapeDtypeStruct(h0.shape, F32)]
    if reverse:
        y_fwd, z = merge
        in_specs = [tok(CONV_CH), tok(LANES), _const_spec(w["dt_bias"].shape), _const_spec(a_row.shape), state_spec,
                    tok(SSD_DI), tok(SSD_DI), _const_spec(w["d_skip"].shape), _const_spec(w["ssd_norm"].shape)]
        args = [xbc, dtm, w["dt_bias"], a_row, h0, y_fwd, z, w["d_skip"], w["ssd_norm"]]
        scratch = [pltpu.VMEM((CHUNK, SSD_DI), F32)]
    else:
        in_specs = [
            tok(CONV_CH),
            pl.BlockSpec((1, HALO, CONV_CH), lambda b, c: (b, jnp.maximum(c * per - 1, 0), 0)),
            pl.BlockSpec((1, HALO, CONV_CH), lambda b, c: (b, jnp.minimum((c + 1) * per, nhb - 1), 0)),
            tok(LANES),
            _const_spec(w["conv_w"].shape),
            _const_spec(w["conv_b"].shape),
            _const_spec(w["dt_bias"].shape),
            _const_spec(a_row.shape),
            state_spec,
        ]
        args = [xbc, xbc, xbc, dtm, w["conv_w"], w["conv_b"], w["dt_bias"], a_row, h0]
        scratch = [pltpu.VMEM((CHUNK + 2 * HALO, CONV_CH), F32)]
        y_spec.append(tok(CONV_CH))
        y_shape.append(jax.ShapeDtypeStruct((bsz, n, CONV_CH), F32))
    return pl.pallas_call(
        functools.partial(_ssd_kernel, nc=nc, reverse=reverse, lane0=SSD_HEADS if reverse else 0),
        grid=(bsz, nc),
        in_specs=in_specs,
        out_specs=y_spec,
        out_shape=y_shape,
        scratch_shapes=scratch,
        compiler_params=_params("arbitrary", "arbitrary"),
        name="ssd_rev" if reverse else "ssd_fwd",
    )(*args)


def _flash_kernel(*refs, n_steps, has_ctx):
    if has_ctx:
        (qt_ref, k_ref, vt_ref, kc_ref, vct_ref, o_ref,
         s_ref, p_ref, al_ref, mx_ref, m_ref, acc_ref) = refs
    else:
        qt_ref, k_ref, vt_ref, o_ref, s_ref, p_ref, al_ref, mx_ref, m_ref, acc_ref = refs
    tk = vt_ref.shape[-1]
    dv = vt_ref.shape[-2]
    total = n_steps + (1 if has_ctx else 0)
    qt = qt_ref[0, 0]
    m_ref[...] = jnp.full(m_ref.shape, -jnp.inf, F32)
    acc_ref[...] = jnp.zeros(acc_ref.shape, F32)

    def is_ctx(step):
        return has_ctx and isinstance(step, int) and step == n_steps

    def rows_of(step):
        return kc_ref.shape[2] if is_ctx(step) else tk

    def scores(step, slot):
        if is_ctx(step):
            s = _dot(kc_ref[0, 0], qt)
        else:
            start = step * tk if isinstance(step, int) else pl.multiple_of(step * tk, tk)
            s = _dot(k_ref[0, 0, pl.ds(start, tk), :], qt)
        s_ref[slot, 0:rows_of(step)] = s
        mx_ref[slot] = jnp.max(s, axis=0, keepdims=True)

    def softmax(step, slot):
        rows = rows_of(step)
        s = s_ref[slot, 0:rows]
        m_old = m_ref[...]
        m_new = jnp.maximum(m_old, mx_ref[slot])
        m_ref[...] = m_new
        al_ref[slot] = jnp.exp2(m_old - m_new)
        p_ref[slot, 0:rows] = jnp.exp2(s - m_new).astype(BF16)

    def values(step, slot):
        vt = vct_ref[0, 0, 0] if is_ctx(step) else vt_ref[0, 0, step]
        vt1 = jnp.concatenate([vt, jnp.ones((BF16_ROWS, vt.shape[1]), BF16)], axis=0)
        acc_ref[...] = al_ref[slot] * acc_ref[...] + _dot(vt1, p_ref[slot, 0:rows_of(step)])

    def tick(t, parity):
        static = isinstance(t, int)
        if not static or 0 <= t - 2 * FLASH_LAG < total:
            values(t - 2 * FLASH_LAG, parity)
        if not static or 0 <= t - FLASH_LAG < total:
            softmax(t - FLASH_LAG, parity)
        if not static or t < total:
            scores(t, parity)

    first = 2 * FLASH_LAG
    n_groups = max(n_steps - first, 0) // FLASH_GROUP
    for t in range(first):
        tick(t, t % 2)
    if n_groups > 0:
        def group(i, carry):
            for u in range(FLASH_GROUP):
                tick(first + FLASH_GROUP * i + u, u % 2)
            return carry

        lax.fori_loop(0, n_groups, group, 0)
    for t in range(first + FLASH_GROUP * n_groups, total + 2 * FLASH_LAG):
        tick(t, t % 2)
    o_ref[0] = (acc_ref[0:dv, :] / acc_ref[dv:dv + 1, :]).T.astype(BF16)


def _flash(qt, k, vt, ctx_kv=None):
    bsz, nh, dqk, n = qt.shape
    n_steps, dv, tk = vt.shape[2:]
    tq = min(FLASH_TQ, n)
    has_ctx = ctx_kv is not None
    whole = lambda x: pl.BlockSpec((1, 1) + x.shape[2:], lambda b, h, i: (b, h) + (0,) * (x.ndim - 2))
    in_specs = [pl.BlockSpec((1, 1, dqk, tq), lambda b, h, i: (b, h, 0, i)), whole(k), whole(vt)]
    args = [qt, k, vt]
    if has_ctx:
        kc, vct = ctx_kv
        assert vct.shape[2] == 1 and kc.shape[2] <= tk
        in_specs += [whole(kc), whole(vct)]
        args += [kc, vct]
    return pl.pallas_call(
        functools.partial(_flash_kernel, n_steps=n_steps, has_ctx=has_ctx),
        grid=(bsz, nh, n // tq),
        in_specs=in_specs,
        out_specs=pl.BlockSpec((1, tq, dv), lambda b, h, i: (b, i, h)),
        out_shape=jax.ShapeDtypeStruct((bsz, n, nh * dv), BF16),
        scratch_shapes=[
            pltpu.VMEM((2, tk, tq), F32), pltpu.VMEM((2, tk, tq), BF16),
            pltpu.VMEM((2, 1, tq), F32), pltpu.VMEM((2, 1, tq), F32),
            pltpu.VMEM((1, tq), F32), pltpu.VMEM((dv + BF16_ROWS, tq), F32),
        ],
        compiler_params=_params("arbitrary", "arbitrary", "arbitrary"),
        name="mla_flash",
    )(*args)


def _odd_proj_kernel(h_ref, m_ref, cos_ref, sin_ref, wq_ref, wk_ref, wv_ref, gq_ref, gk_ref,
                     qt_ref, k_ref, vt_ref):
    u = _modulate(h_ref[0], m_ref[0, 3:4, :], m_ref[0, 4:5, :]).astype(BF16)
    cos = cos_ref[...]
    sin = sin_ref[...]
    lo = _lane_lo(cos.shape)
    first_quarter = (lax.broadcasted_iota(jnp.int32, cos.shape, 1) % (GQA_HD // 2)) < (GQA_HD // 4)

    def norm_rope(x2, g2):
        sq = x2 * x2
        s_lo = jnp.sum(jnp.where(lo, sq, 0.0), axis=-1, keepdims=True) * (1.0 / GQA_HD)
        s_hi = jnp.sum(jnp.where(lo, 0.0, sq), axis=-1, keepdims=True) * (1.0 / GQA_HD)
        y = x2 * jnp.where(lo, lax.rsqrt(s_lo + EPS), lax.rsqrt(s_hi + EPS)) * g2
        rot = jnp.where(first_quarter, pltpu.roll(y, LANES - GQA_HD // 4, 1), pltpu.roll(y, GQA_HD // 4, 1))
        return y * cos + rot * sin

    q = _dot(u, wq_ref[...])
    k = _dot(u, wk_ref[...])
    v = _dot(u, wv_ref[...])
    for j in range(q.shape[1] // LANES):
        sl = slice(j * LANES, (j + 1) * LANES)
        qt_ref[0, sl, :] = (norm_rope(q[:, sl], gq_ref[...]) * GQA_PRESCALE).T.astype(BF16)
    for j in range(k.shape[1] // LANES):
        sl = slice(j * LANES, (j + 1) * LANES)
        k_ref[0, :, sl] = norm_rope(k[:, sl], gk_ref[...]).astype(BF16)
        vt_ref[0, sl, :] = v[:, sl].T.astype(BF16)


def _odd_proj(h, mod_l, row_of, cos, sin, w):
    bsz, n, d = h.shape
    tm = _token_tile(n)
    weights = [w["wq"], w["wk"], w["wv"], w["gq"], w["gk"]]
    tok = lambda width: pl.BlockSpec((1, tm, width), lambda b, i: (b, i, 0))
    nq = GQA_HQ * GQA_HD
    nkv = GQA_HKV * GQA_HD
    return pl.pallas_call(
        _odd_proj_kernel,
        grid=(bsz, n // tm),
        in_specs=[
            tok(d),
            pl.BlockSpec((1, N_MOD, d), lambda b, i: (row_of(b), 0, 0)),
            pl.BlockSpec((tm, LANES), lambda b, i: (i, 0)),
            pl.BlockSpec((tm, LANES), lambda b, i: (i, 0)),
        ] + [_const_spec(x.shape) for x in weights],
        out_specs=[
            pl.BlockSpec((1, nq, tm), lambda b, i: (b, 0, i)),
            tok(nkv),
            pl.BlockSpec((1, nkv, tm), lambda b, i: (b, 0, i)),
        ],
        out_shape=[
            jax.ShapeDtypeStruct((bsz, nq, n), BF16),
            jax.ShapeDtypeStruct((bsz, n, nkv), BF16),
            jax.ShapeDtypeStruct((bsz, nkv, n), BF16),
        ],
        compiler_params=_params("arbitrary", "arbitrary"),
        name="odd_proj",
    )(h, mod_l, cos, sin, *weights)


def _band_bias(tq):
    j = np.arange(tq + 2 * WINDOW)[:, None]
    t = (np.arange(GQA_REP * tq) % tq)[None, :]
    return np.where(np.abs(j - WINDOW - t) <= WINDOW, 0.0, -np.inf).astype(np.float32)


def _window_kernel(*refs, nb, local):
    if local:
        (sink_ref, qt_ref, kp_ref, kc_ref, kn_ref, vp_ref, vc_ref, vn_ref, kx_ref, vx_ref, bias_ref,
         o_ref) = refs
    else:
        sink_ref, qt_ref, kx_ref, vx_ref, o_ref = refs
    i = pl.program_id(1)
    tq = qt_ref.shape[2]
    cols = GQA_REP * tq
    top = lax.broadcasted_iota(jnp.int32, (LANES, cols), 0) < GQA_HD
    if local:
        span = tq + 2 * WINDOW
        ninf = jnp.float32(-jnp.inf)
        bias = jnp.concatenate([
            jnp.where(i > 0, bias_ref[0:WINDOW, :], ninf),
            bias_ref[WINDOW:WINDOW + tq, :],
            jnp.where(i < nb - 1, bias_ref[WINDOW + tq:span, :], ninf)], axis=0)
    for gp in range(GQA_HKV // 2):
        fsl = slice(gp * LANES, (gp + 1) * LANES)
        if local:
            k_all = jnp.concatenate(
                [kp_ref[0, :, fsl], kc_ref[0, :, fsl], kn_ref[0, :, fsl], kx_ref[0, :, fsl]], axis=0)
            vt_all = jnp.concatenate(
                [vp_ref[0, fsl, :], vc_ref[0, fsl, :], vn_ref[0, fsl, :], vx_ref[0, fsl, :]], axis=1)
        else:
            k_all = kx_ref[0, :, fsl]
            vt_all = vx_ref[0, fsl, :]
        vt1 = jnp.concatenate([vt_all, jnp.ones((BF16_ROWS, vt_all.shape[1]), BF16)], axis=0)
        qs = jnp.concatenate(
            [qt_ref[0, (gp * GQA_REP + r) * LANES:(gp * GQA_REP + r + 1) * LANES, :] for r in range(GQA_REP)],
            axis=1)
        halves = []
        for half in range(2):
            keep = top if half == 0 else jnp.logical_not(top)
            qh = jnp.where(keep, qs, jnp.zeros_like(qs))
            sink = jnp.concatenate(
                [jnp.full((1, tq), sink_ref[(2 * gp + half) * GQA_REP + r] * LOG2E, F32) for r in range(GQA_REP)],
                axis=1)
            s = _dot(k_all, qh)
            if local:
                parts = [s[0:span] + bias, s[span:]]
            else:
                parts = [s]
            m = sink
            for part in parts:
                m = jnp.maximum(m, jnp.max(part, axis=0, keepdims=True))
            p = jnp.concatenate([jnp.exp2(part - m).astype(BF16) for part in parts], axis=0)
            acc = _dot(vt1, p)
            halves.append(acc[0:LANES] / (acc[LANES:LANES + 1] + jnp.exp2(sink - m)))
        ot = jnp.where(top, halves[0], halves[1])
        for r in range(GQA_REP):
            col = gp * GQA_REP + r
            o_ref[0, :, col * LANES:(col + 1) * LANES] = ot[:, r * tq:(r + 1) * tq].T.astype(BF16)


def _window_attn(qt, kx, vxt, sink, local_kv=None):
    bsz, nq, n = qt.shape
    nkv = kx.shape[2]
    tq = WINDOW
    nb = n // tq
    local = local_kv is not None
    in_specs = [pl.BlockSpec(memory_space=pltpu.SMEM), pl.BlockSpec((1, nq, tq), lambda b, i: (b, 0, i))]
    args = [sink, qt]
    if local:
        k, vt = local_kv
        prev = lambda i: jnp.maximum(i - 1, 0)
        nxt = lambda i: jnp.minimum(i + 1, nb - 1)
        in_specs += [pl.BlockSpec((1, tq, nkv), lambda b, i: (b, prev(i), 0)),
                     pl.BlockSpec((1, tq, nkv), lambda b, i: (b, i, 0)),
                     pl.BlockSpec((1, tq, nkv), lambda b, i: (b, nxt(i), 0)),
                     pl.BlockSpec((1, nkv, tq), lambda b, i: (b, 0, prev(i))),
                     pl.BlockSpec((1, nkv, tq), lambda b, i: (b, 0, i)),
                     pl.BlockSpec((1, nkv, tq), lambda b, i: (b, 0, nxt(i)))]
        args += [k, k, k, vt, vt, vt]
    in_specs += [pl.BlockSpec((1,) + kx.shape[1:], lambda b, i: (b, 0, 0)),
                 pl.BlockSpec((1,) + vxt.shape[1:], lambda b, i: (b, 0, 0))]
    args += [kx, vxt]
    if local:
        bias = jnp.asarray(_band_bias(tq))
        in_specs.append(_const_spec(bias.shape))
        args.append(bias)
    return pl.pallas_call(
        functools.partial(_window_kernel, nb=nb, local=local),
        grid=(bsz, nb),
        in_specs=in_specs,
        out_specs=pl.BlockSpec((1, tq, nq), lambda b, i: (b, i, 0)),
        out_shape=jax.ShapeDtypeStruct((bsz, n, nq), BF16),
        compiler_params=_params("arbitrary", "arbitrary"),
        name="window_attn" if local else "sink_attn",
    )(*args)


_ROT_SRC = np.concatenate([np.arange(16, 32), np.arange(0, 16), np.arange(48, 64), np.arange(32, 48)])
_ROT_SIGN = np.concatenate([-np.ones(16), np.ones(16), -np.ones(16), np.ones(16)]).astype(np.float32)


def _rope_tables(n):
    quarter = MLA_DR // 4
    freqs = ROPE_BASE ** (-jnp.arange(quarter, dtype=F32) / quarter)
    t = jnp.arange(n)
    row = (t // GRID_W).astype(F32)
    col = (t % GRID_W).astype(F32)
    ang_r = row[:, None] * freqs
    ang_c = col[:, None] * freqs
    cos = jnp.concatenate([jnp.cos(ang_r), jnp.cos(ang_r), jnp.cos(ang_c), jnp.cos(ang_c)], axis=-1)
    sin = jnp.concatenate([jnp.sin(ang_r), jnp.sin(ang_r), jnp.sin(ang_c), jnp.sin(ang_c)], axis=-1)
    return cos, sin


def _ffn_weights(w_in, w_out):
    d = w_in.shape[0]
    wg = w_in[:, :FFN_DIM].reshape(d, N_FFN_CHUNKS, FFN_CHUNK).transpose(1, 0, 2).astype(BF16)
    wu = w_in[:, FFN_DIM:].reshape(d, N_FFN_CHUNKS, FFN_CHUNK).transpose(1, 0, 2).astype(BF16)
    return wg, wu, w_out.astype(BF16)


def _pad_lanes(row, offset=0):
    out = jnp.zeros((1, LANES), F32)
    return out.at[0, offset:offset + row.shape[0]].set(row.astype(F32))


def _even_weights(w_in, conv_w, conv_b, dt_bias, a_log, d_skip, ssd_norm, qa_norm, w_qb, kv_norm,
                  w_kvb, qn_g, qr_g, kn_g, kr_g, w_out):
    d = w_in.shape[0]
    sign = jnp.asarray(_ROT_SIGN)
    w_dt = w_in[:, _E1:_E2]
    w_kr = w_in[:, _E4:]
    wm = jnp.concatenate([w_dt, jnp.zeros((d, LANES - 2 * SSD_HEADS), F32), w_kr, w_kr[:, _ROT_SRC] * sign], axis=1)
    wq = w_qb.reshape(MLA_Q_RANK, MLA_HEADS, MLA_DN + MLA_DR)
    wq_r = wq[:, :, MLA_DN:]
    wqr = jnp.concatenate([wq_r, wq_r[:, :, _ROT_SRC] * sign], axis=-1).reshape(MLA_Q_RANK, MLA_HEADS * LANES)
    wkv = w_kvb.reshape(MLA_KV_RANK, MLA_HEADS, MLA_DN + MLA_DV)
    row = lambda g: g.reshape(1, -1).astype(F32)
    return {
        "wz": w_in[:, :_E0].astype(BF16),
        "wx": w_in[:, _E0:_E1].astype(BF16),
        "wm": wm.astype(BF16),
        "wqa": w_in[:, _E2:_E3].astype(BF16),
        "wkv": w_in[:, _E3:_E4].astype(BF16),
        "wqn": wq[:, :, :MLA_DN].reshape(MLA_Q_RANK, MLA_HEADS * MLA_DN).astype(BF16),
        "wqr": wqr.astype(BF16),
        "wkn": wkv[:, :, :MLA_DN].reshape(MLA_KV_RANK, MLA_HEADS * MLA_DN).astype(BF16),
        "wv": wkv[:, :, MLA_DN:].reshape(MLA_KV_RANK, MLA_HEADS * MLA_DV).astype(BF16),
        "gqa": row(qa_norm), "gkv": row(kv_norm), "gqn": row(qn_g), "gkn": row(kn_g),
        "gqr": row(jnp.concatenate([qr_g, qr_g[_ROT_SRC]])),
        "gkr": row(jnp.concatenate([kr_g, kr_g[_ROT_SRC]])),
        "conv_w": conv_w.astype(F32), "conv_b": row(conv_b),
        "dt_bias": _pad_lanes(dt_bias.reshape(-1)),
        "a_fwd": _pad_lanes(-jnp.exp(a_log[0].astype(F32)), 0),
        "a_rev": _pad_lanes(-jnp.exp(a_log[1].astype(F32)), SSD_HEADS),
        "d_skip": row(jnp.repeat(d_skip, SSD_P)), "ssd_norm": row(ssd_norm),
        "wo_ssd": w_out[:SSD_DI].astype(BF16), "wo_mla": w_out[SSD_DI:].astype(BF16),
    }


def _gqa_col_perm():
    perm = np.zeros(GQA_HQ * GQA_HD, np.int32)
    for gp in range(GQA_HKV // 2):
        for r in range(GQA_REP):
            for half in range(2):
                head = (2 * gp + half) * GQA_REP + r
                new = ((gp * GQA_REP + r) * 2 + half) * GQA_HD
                perm[new:new + GQA_HD] = np.arange(head * GQA_HD, (head + 1) * GQA_HD)
    return perm


_GQA_PERM = _gqa_col_perm()


def _odd_weights(w_in, q_g, k_g, sink, w_out):
    nq = GQA_HQ * GQA_HD
    nkv = GQA_HKV * GQA_HD
    tile2 = lambda g: jnp.concatenate([g, g]).reshape(1, LANES).astype(F32)
    return {
        "wq": w_in[:, :nq][:, _GQA_PERM].astype(BF16),
        "wk": w_in[:, nq:nq + nkv].astype(BF16),
        "wv": w_in[:, nq + nkv:].astype(BF16),
        "gq": tile2(q_g), "gk": tile2(k_g),
        "sink": sink.astype(F32),
        "wo": w_out[_GQA_PERM, :].astype(BF16),
    }


def kernel(x, c, ctx, c_ctx, w_mod, b_mod, w_ff1_in, w_ff1_out, w_ff2_in, w_ff2_out, w_in_e, conv_w, conv_b, dt_bias, a_log, d_skip, ssd_norm, mla_qa_norm, w_qb, mla_kv_norm, w_kvb, mla_qn_norm, mla_qr_norm, mla_kn_norm, mla_kr_norm, w_out_e, w_in_o, gqa_q_norm, gqa_k_norm, sink, w_out_o):
    bsz, n, d = x.shape
    lc = ctx.shape[1]
    depth = w_mod.shape[0]
    assert d == D_MODEL and bsz + 1 <= MOD_ROWS and n % TOKEN_TILE == 0 and lc % CHUNK == 0

    cond = jnp.zeros((MOD_ROWS, d), F32).at[:bsz].set(c).at[bsz].set(c_ctx)
    mod = _mod_vectors(cond, w_mod, b_mod).reshape(depth, MOD_ROWS, N_MOD, d)
    lat_row = lambda b: b
    ctx_row = lambda b: bsz

    cos64, sin64 = _rope_tables(n)
    tab_lat = jnp.concatenate([cos64, sin64], axis=-1)
    tab_ctx = jnp.concatenate([jnp.ones((lc, MLA_DR), F32), jnp.zeros((lc, MLA_DR), F32)], axis=-1)
    sign = jnp.asarray(_ROT_SIGN)
    cos_lat = jnp.concatenate([cos64, cos64], axis=-1)
    sin_lat = jnp.concatenate([sin64 * sign, sin64 * sign], axis=-1)
    cos_ctx = jnp.ones((lc, LANES), F32)
    sin_ctx = jnp.zeros((lc, LANES), F32)

    h, hc = x, ctx
    for l in range(depth):
        need_ctx = l < depth - 1
        mod_l = mod[l]
        ff1 = _ffn_weights(w_ff1_in[l], w_ff1_out[l])
        ff2 = _ffn_weights(w_ff2_in[l], w_ff2_out[l])
        h = _ffn(h, mod_l, lat_row, *ff1, k0=0)
        hc = _ffn(hc, mod_l, ctx_row, *ff1, k0=0)
        if l % 2 == 0:
            e = l // 2
            w = _even_weights(w_in_e[e], conv_w[e], conv_b[e], dt_bias[e], a_log[e], d_skip[e], ssd_norm[e],
                              mla_qa_norm[e], w_qb[e], mla_kv_norm[e], w_kvb[e], mla_qn_norm[e],
                              mla_qr_norm[e], mla_kn_norm[e], mla_kr_norm[e], w_out_e[e])
            zc, xbcc, dtc, qc, kc, vc = _even_proj(hc, mod_l, ctx_row, tab_ctx, w)
            z, xbc, dtm, q, k, v = _even_proj(h, mod_l, lat_row, tab_lat, w)
            s0 = jnp.zeros((bsz, SSD_HEADS // 2, SSD_N, LANES), F32)
            yfc, s_fc, xac = _ssd(xbcc, dtc, s0, w, reverse=False)
            ysc, s_bc = _ssd(xac, dtc, s0, w, reverse=True, merge=(yfc, zc))
            yf, _, xa = _ssd(xbc, dtm, s_fc, w, reverse=False)
            ys, _ = _ssd(xa, dtm, s_bc, w, reverse=True, merge=(yf, z))
            o = _flash(q, k, v, ctx_kv=(kc, vc))
            mix = ((ys, w["wo_ssd"]), (o, w["wo_mla"]))
            if need_ctx:
                mix_c = ((ysc, w["wo_ssd"]), (_flash(qc, kc, vc), w["wo_mla"]))
        else:
            o_ = l // 2
            w = _odd_weights(w_in_o[o_], gqa_q_norm[o_], gqa_k_norm[o_], sink[o_], w_out_o[o_])
            qc, kc, vc = _odd_proj(hc, mod_l, ctx_row, cos_ctx, sin_ctx, w)
            q, k, v = _odd_proj(h, mod_l, lat_row, cos_lat, sin_lat, w)
            mix = ((_window_attn(q, kc, vc, w["sink"], local_kv=(k, v)), w["wo"]),)
            if need_ctx:
                mix_c = ((_window_attn(qc, kc, vc, w["sink"]), w["wo"]),)
        h = _ffn(h, mod_l, lat_row, *ff2, k0=6, mix=mix)
        if need_ctx:
            hc = _ffn(hc, mod_l, ctx_row, *ff2, k0=6, mix=mix_c)
    return h
```

```python
import functools
import math

import jax
import jax.numpy as jnp
import numpy as np
from jax import lax
from jax.experimental import pallas as pl
from jax.experimental.pallas import tpu as pltpu

F32 = jnp.float32
BF16 = jnp.bfloat16
HIGHEST = lax.Precision.HIGHEST

D_MODEL = 1024
GRID_W = 64
N_MOD = 9
FFN_DIM = 2816
SSD_P = 64
SSD_HEADS = 16
SSD_DI = 1024
SSD_G = 2
SSD_N = 128
D_CONV = 5
CONV_CH = SSD_DI + 2 * SSD_G * SSD_N
CHUNK = 128
MLA_DN = 128
MLA_DR = 64
MLA_DV = 128
MLA_HEADS = 8
MLA_Q_RANK = 384
MLA_KV_RANK = 256
GQA_HD = 64
GQA_HQ = 16
GQA_HKV = 4
GQA_REP = 4
WINDOW = 128
ROPE_BASE = 10000.0
EPS = 1e-6
_E0 = SSD_DI
_E1 = _E0 + CONV_CH
_E2 = _E1 + 2 * SSD_HEADS
_E3 = _E2 + MLA_Q_RANK
_E4 = _E3 + MLA_KV_RANK
E_IN = _E4 + MLA_DR
MLA_SCALE = (MLA_DN + MLA_DR) ** -0.5
GQA_SCALE = GQA_HD ** -0.5
LOG2E = math.log2(math.e)
Q_PRESCALE = MLA_SCALE * LOG2E
GQA_PRESCALE = GQA_SCALE * LOG2E

LANES = 128
SUBLANES = 8
BF16_ROWS = 16
VMEM_LIMIT_BYTES = 56 * 1024 * 1024
MOD_ROWS = 16
FFN_CHUNK = 256
N_FFN_CHUNKS = FFN_DIM // FFN_CHUNK
TOKEN_TILE = 512
FLASH_TQ = 1024
FLASH_LAG = 2
FLASH_GROUP = 6
HALO = SUBLANES


def _params(*sem):
    return pltpu.CompilerParams(dimension_semantics=sem, vmem_limit_bytes=VMEM_LIMIT_BYTES)


def _const_spec(shape):
    nd = len(shape)
    return pl.BlockSpec(shape, lambda *_: (0,) * nd, pipeline_mode=pl.Buffered(1))


def _dot(a, b):
    return jnp.dot(a, b, preferred_element_type=F32)


def _dot_nt(a, b):
    return lax.dot_general(a, b, (((1,), (1,)), ((), ())), preferred_element_type=F32)


def _silu(x):
    return x * jax.nn.sigmoid(x)


def _rms(x):
    return x * lax.rsqrt(jnp.mean(x * x, axis=-1, keepdims=True) + EPS)


def _modulate(h, shift, scale):
    return _rms(h) * (1.0 + scale) + shift


def _lane_lo(shape):
    return (lax.broadcasted_iota(jnp.int32, shape, len(shape) - 1) % LANES) < (LANES // 2)


def _mod_kernel(s_ref, w_ref, b_ref, o_ref):
    s = _silu(s_ref[...])
    o_ref[0] = jnp.dot(s, w_ref[0], preferred_element_type=F32, precision=HIGHEST) + b_ref[0]


def _mod_vectors(cond, w_mod, b_mod):
    depth, d, nm = w_mod.shape
    tn = 1024
    return pl.pallas_call(
        _mod_kernel,
        grid=(depth, nm // tn),
        in_specs=[
            pl.BlockSpec((MOD_ROWS, d), lambda l, j: (0, 0)),
            pl.BlockSpec((1, d, tn), lambda l, j: (l, 0, j)),
            pl.BlockSpec((1, 1, tn), lambda l, j: (l, 0, j)),
        ],
        out_specs=pl.BlockSpec((1, MOD_ROWS, tn), lambda l, j: (l, 0, j)),
        out_shape=jax.ShapeDtypeStruct((depth, MOD_ROWS, nm), F32),
        compiler_params=_params("arbitrary", "arbitrary"),
        name="mod_vectors",
    )(cond, w_mod, b_mod.reshape(depth, 1, nm))


def _ffn_kernel(*refs, k0, n_mix):
    h_ref, m_ref = refs[0], refs[1]
    mix_a = refs[2:2 + n_mix]
    mix_w = refs[2 + n_mix:2 + 2 * n_mix]
    wg_ref, wu_ref, wo_ref, o_ref, a_ref = refs[2 + 2 * n_mix:]
    h = h_ref[0]
    if n_mix:
        y = _dot(mix_a[0][0], mix_w[0][...])
        for ar, wr in zip(mix_a[1:], mix_w[1:]):
            y = y + _dot(ar[0], wr[...])
        h = h + m_ref[0, 5:6, :] * y
    xm = _modulate(h, m_ref[0, k0:k0 + 1, :], m_ref[0, k0 + 1:k0 + 2, :]).astype(BF16)
    for j in range(N_FFN_CHUNKS):
        g = _dot(xm, wg_ref[j])
        u = _dot(xm, wu_ref[j])
        a_ref[:, j * FFN_CHUNK:(j + 1) * FFN_CHUNK] = (_silu(g) * u).astype(BF16)
    y = _dot(a_ref[...], wo_ref[...])
    o_ref[0] = h + (0.5 * m_ref[0, k0 + 2:k0 + 3, :]) * y


def _token_tile(n):
    return min(TOKEN_TILE, n)


def _ffn(h, mod_l, row_of, wg, wu, wo, k0, mix=()):
    bsz, n, d = h.shape
    tm = _token_tile(n)
    acts = [a for a, _ in mix]
    mix_w = [x for _, x in mix]
    return pl.pallas_call(
        functools.partial(_ffn_kernel, k0=k0, n_mix=len(mix)),
        grid=(bsz, n // tm),
        in_specs=[
            pl.BlockSpec((1, tm, d), lambda b, i: (b, i, 0)),
            pl.BlockSpec((1, N_MOD, d), lambda b, i: (row_of(b), 0, 0)),
        ] + [pl.BlockSpec((1, tm, a.shape[-1]), lambda b, i: (b, i, 0)) for a in acts]
        + [_const_spec(x.shape) for x in mix_w + [wg, wu, wo]],
        out_specs=pl.BlockSpec((1, tm, d), lambda b, i: (b, i, 0)),
        out_shape=jax.ShapeDtypeStruct(h.shape, F32),
        scratch_shapes=[pltpu.VMEM((tm, FFN_DIM), BF16)],
        compiler_params=_params("arbitrary", "arbitrary"),
        name="ffn_mix" if mix else "ffn",
    )(h, mod_l, *acts, *mix_w, wg, wu, wo)


def _norm_rope_pair(t2, g2, tab):
    lo = _lane_lo(t2.shape)
    ss = jnp.sum(jnp.where(lo, t2 * t2, 0.0), axis=-1, keepdims=True) * (1.0 / MLA_DR)
    y = t2 * lax.rsqrt(ss + EPS) * g2 * tab
    return y + pltpu.roll(y, LANES // 2, 1)


def _even_proj_kernel(h_ref, m_ref, tab_ref, cost_ref, sint_ref, wz_ref, wx_ref, wm_ref, wqa_ref, wkv_ref,
                      wqn_ref, wqr_ref, wkn_ref, wv_ref, gqa_ref, gkv_ref, gqnt_ref, gqrt_ref, gkn_ref,
                      gkr_ref, z_ref, xbc_ref, dt_ref, qt_ref, k_ref, vt_ref):
    u = _modulate(h_ref[0], m_ref[0, 3:4, :], m_ref[0, 4:5, :]).astype(BF16)
    tab = tab_ref[...]
    cost = cost_ref[...]
    sint = sint_ref[...]
    qd = MLA_DR // 4
    z_ref[0] = _dot(u, wz_ref[...])
    xbc_ref[0] = _dot(u, wx_ref[...])
    misc = _dot(u, wm_ref[...])
    dt_ref[0] = misc[:, :LANES]
    kr = _norm_rope_pair(misc[:, LANES:], gkr_ref[...], tab)[:, :MLA_DR].astype(BF16)
    qa = (_rms(_dot(u, wqa_ref[...])) * gqa_ref[...]).astype(BF16)
    qn = _dot(qa, wqn_ref[...])
    qr = _dot(qa, wqr_ref[...])
    ckv = (_rms(_dot(u, wkv_ref[...])) * gkv_ref[...]).astype(BF16)
    kn = _dot(ckv, wkn_ref[...])
    v = _dot(ckv, wv_ref[...])
    for hd in range(MLA_HEADS):
        sl = slice(hd * LANES, (hd + 1) * LANES)
        qnt = qn[:, sl].T
        qnt = qnt * lax.rsqrt(jnp.mean(qnt * qnt, axis=0, keepdims=True) + EPS) * gqnt_ref[...]
        qt_ref[0, hd, 0:MLA_DN, :] = (qnt * Q_PRESCALE).astype(BF16)
        k_ref[0, hd, :, 0:MLA_DN] = (_rms(kn[:, sl]) * gkn_ref[...]).astype(BF16)
        k_ref[0, hd, :, MLA_DN:MLA_DN + MLA_DR] = kr
        vt_ref[0, hd, 0] = v[:, sl].T.astype(BF16)
    for pair in range(MLA_HEADS // 2):
        qrt = qr[:, pair * LANES:(pair + 1) * LANES].T
        for half in range(2):
            y = qrt[half * MLA_DR:(half + 1) * MLA_DR]
            y = y * lax.rsqrt(jnp.mean(y * y, axis=0, keepdims=True) + EPS) * gqrt_ref[...]
            rot = jnp.concatenate([y[qd:2 * qd], y[0:qd], y[3 * qd:4 * qd], y[2 * qd:3 * qd]], axis=0)
            out = (y * cost + rot * sint) * Q_PRESCALE
            qt_ref[0, 2 * pair + half, MLA_DN:MLA_DN + MLA_DR, :] = out.astype(BF16)


def _even_proj(h, mod_l, row_of, tab, cos_t, sin_t, w):
    bsz, n, d = h.shape
    tm = _token_tile(n)
    gqnt = jnp.broadcast_to(w["gqn"].reshape(MLA_DN, 1), (MLA_DN, tm))
    gqrt = jnp.broadcast_to(w["gqr"].reshape(MLA_DR, 1), (MLA_DR, tm))
    weights = [w["wz"], w["wx"], w["wm"], w["wqa"], w["wkv"], w["wqn"], w["wqr"], w["wkn"], w["wv"],
               w["gqa"], w["gkv"], gqnt, gqrt, w["gkn"], w["gkr"]]
    dqk = MLA_DN + MLA_DR
    tok = lambda width: pl.BlockSpec((1, tm, width), lambda b, i: (b, i, 0))
    return pl.pallas_call(
        _even_proj_kernel,
        grid=(bsz, n // tm),
        in_specs=[
            tok(d),
            pl.BlockSpec((1, N_MOD, d), lambda b, i: (row_of(b), 0, 0)),
            pl.BlockSpec((tm, LANES), lambda b, i: (i, 0)),
            pl.BlockSpec((MLA_DR, tm), lambda b, i: (0, i)),
            pl.BlockSpec((MLA_DR, tm), lambda b, i: (0, i)),
        ] + [_const_spec(x.shape) for x in weights],
        out_specs=[
            tok(SSD_DI), tok(CONV_CH), tok(LANES),
            pl.BlockSpec((1, MLA_HEADS, dqk, tm), lambda b, i: (b, 0, 0, i)),
            pl.BlockSpec((1, MLA_HEADS, tm, dqk), lambda b, i: (b, 0, i, 0)),
            pl.BlockSpec((1, MLA_HEADS, 1, MLA_DV, tm), lambda b, i: (b, 0, i, 0, 0)),
        ],
        out_shape=[
            jax.ShapeDtypeStruct((bsz, n, SSD_DI), F32),
            jax.ShapeDtypeStruct((bsz, n, CONV_CH), F32),
            jax.ShapeDtypeStruct((bsz, n, LANES), F32),
            jax.ShapeDtypeStruct((bsz, MLA_HEADS, dqk, n), BF16),
            jax.ShapeDtypeStruct((bsz, MLA_HEADS, n, dqk), BF16),
            jax.ShapeDtypeStruct((bsz, MLA_HEADS, n // tm, MLA_DV, tm), BF16),
        ],
        compiler_params=_params("arbitrary", "arbitrary"),
        name="even_proj",
    )(h, mod_l, tab, cos_t, sin_t, *weights)


def _ssd_kernel(*refs, nc, reverse, lane0):
    if reverse:
        (xa_ref, dt_ref, dtb_ref, a_ref, h0_ref, yf_ref, z_ref, dsk_ref, nrm_ref,
         y_ref, st_ref, gat_ref) = refs
    else:
        (xm_ref, xp_ref, xn_ref, dt_ref, cw_ref, cb_ref, dtb_ref, a_ref, h0_ref,
         y_ref, st_ref, xa_ref, ext_ref) = refs
    c = pl.program_id(1)

    @pl.when(c == 0)
    def _():
        st_ref[...] = h0_ref[...]

    if reverse:
        xbc = xa_ref[0]
    else:
        ext_ref[0:HALO, :] = jnp.where(c == 0, 0.0, xp_ref[0])
        ext_ref[HALO:HALO + CHUNK, :] = xm_ref[0]
        ext_ref[HALO + CHUNK:2 * HALO + CHUNK, :] = jnp.where(c == nc - 1, 0.0, xn_ref[0])
        acc = cb_ref[...]
        for k in range(D_CONV):
            off = HALO - D_CONV // 2 + k
            acc = acc + cw_ref[k:k + 1, :] * ext_ref[off:off + CHUNK, :]
        xbc = _silu(acc)
        xa_ref[0] = xbc
    xs = xbc[:, :SSD_DI]

    xr = dt_ref[0] + dtb_ref[...]
    dt = jnp.maximum(xr, 0.0) + jnp.log1p(jnp.exp(-jnp.abs(xr)))
    dta = dt * a_ref[...]
    ri = lax.broadcasted_iota(jnp.int32, (CHUNK, CHUNK), 0)
    ci = lax.broadcasted_iota(jnp.int32, (CHUNK, CHUNK), 1)
    tri = (ci >= ri) if reverse else (ri >= ci)
    acs = jnp.dot(tri.astype(F32), dta, preferred_element_type=F32, precision=HIGHEST)
    acs_t = acs.T
    dt_t = dt.T
    edge = 0 if reverse else CHUNK - 1
    lo = _lane_lo((CHUNK, LANES))
    lo_row = _lane_lo((1, LANES))

    cb_g, bt_g, c_g = [], [], []
    for g in range(SSD_G):
        bg = xbc[:, SSD_DI + g * SSD_N:SSD_DI + (g + 1) * SSD_N]
        cg = xbc[:, SSD_DI + SSD_G * SSD_N + g * SSD_N:SSD_DI + SSD_G * SSD_N + (g + 1) * SSD_N]
        cgb = cg.astype(BF16)
        cb_g.append(_dot_nt(cgb, bg.astype(BF16)))
        bt_g.append(bg.T)
        c_g.append(cgb)

    def head_terms(j, g):
        col = acs[:, j:j + 1]
        row = acs_t[j:j + 1, :]
        dtrow = dt_t[j:j + 1, :]
        lmat = jnp.exp(jnp.where(tri, col - row, -jnp.inf))
        gmat = (cb_g[g] * lmat * dtrow).astype(BF16)
        tot = acs_t[j:j + 1, edge:edge + 1]
        bw = (bt_g[g] * (jnp.exp(tot - row) * dtrow)).astype(BF16)
        return gmat, bw, jnp.exp(col), jnp.exp(tot)

    for k in range(SSD_HEADS // 2):
        g = (2 * k) // (SSD_HEADS // SSD_G)
        xpair = xs[:, k * LANES:(k + 1) * LANES]
        rhs = jnp.concatenate([jnp.where(lo, xpair, 0.0), jnp.where(lo, 0.0, xpair)], axis=0).astype(BF16)
        ga, bwa, ea, da = head_terms(lane0 + 2 * k, g)
        gb, bwb, eb, db = head_terms(lane0 + 2 * k + 1, g)
        s_in = st_ref[0, k]
        y = (_dot(jnp.concatenate([ga, gb], axis=1), rhs)
             + _dot(c_g[g], s_in.astype(BF16)) * jnp.where(lo, ea, eb))
        st_ref[0, k] = s_in * jnp.where(lo_row, da, db) + _dot(jnp.concatenate([bwa, bwb], axis=1), rhs)
        sl = slice(k * LANES, (k + 1) * LANES)
        if reverse:
            zt = z_ref[0, :, sl]
            gat_ref[:, sl] = (yf_ref[0, :, sl] + y + dsk_ref[:, sl] * xpair) * _silu(zt)
        else:
            y_ref[0, :, sl] = y

    if reverse:
        gw = SSD_DI // SSD_G
        for g in range(SSD_G):
            seg = gat_ref[:, g * gw:(g + 1) * gw]
            y_ref[0, :, g * gw:(g + 1) * gw] = (_rms(seg) * nrm_ref[:, g * gw:(g + 1) * gw]).astype(BF16)


def _ssd(xbc, dtm, h0, w, reverse, merge=None):
    bsz, n, _ = xbc.shape
    nc = n // CHUNK
    per = CHUNK // HALO
    nhb = n // HALO
    cidx = (lambda c: nc - 1 - c) if reverse else (lambda c: c)
    tok = lambda width: pl.BlockSpec((1, CHUNK, width), lambda b, c: (b, cidx(c), 0))
    state_spec = pl.BlockSpec((1, SSD_HEADS // 2, SSD_N, LANES), lambda b, c: (b, 0, 0, 0))
    a_row = w["a_rev" if reverse else "a_fwd"]
    y_spec = [tok(SSD_DI), state_spec]
    y_shape = [jax.ShapeDtypeStruct((bsz, n, SSD_DI), BF16 if reverse else F32), jax.ShapeDtypeStruct(h0.shape, F32)]
    if reverse:
        y_fwd, z = merge
        in_specs = [tok(CONV_CH), tok(LANES), _const_spec(w["dt_bias"].shape), _const_spec(a_row.shape), state_spec,
                    tok(SSD_DI), tok(SSD_DI), _const_spec(w["d_skip"].shape), _const_spec(w["ssd_norm"].shape)]
        args = [xbc, dtm, w["dt_bias"], a_row, h0, y_fwd, z, w["d_skip"], w["ssd_norm"]]
        scratch = [pltpu.VMEM((CHUNK, SSD_DI), F32)]
    else:
        in_specs = [
            tok(CONV_CH),
            pl.BlockSpec((1, HALO, CONV_CH), lambda b, c: (b, jnp.maximum(c * per - 1, 0), 0)),
            pl.BlockSpec((1, HALO, CONV_CH), lambda b, c: (b, jnp.minimum((c + 1) * per, nhb - 1), 0)),
            tok(LANES),
            _const_spec(w["conv_w"].shape),
            _const_spec(w["conv_b"].shape),
            _const_spec(w["dt_bias"].shape),
            _const_spec(a_row.shape),
            state_spec,
        ]
        args = [xbc, xbc, xbc, dtm, w["conv_w"], w["conv_b"], w["dt_bias"], a_row, h0]
        scratch = [pltpu.VMEM((CHUNK + 2 * HALO, CONV_CH), F32)]
        y_spec.append(tok(CONV_CH))
        y_shape.append(jax.ShapeDtypeStruct((bsz, n, CONV_CH), F32))
    return pl.pallas_call(
        functools.partial(_ssd_kernel, nc=nc, reverse=reverse, lane0=SSD_HEADS if reverse else 0),
        grid=(bsz, nc),
        in_specs=in_specs,
        out_specs=y_spec,
        out_shape=y_shape,
        scratch_shapes=scratch,
        compiler_params=_params("arbitrary", "arbitrary"),
        name="ssd_rev" if reverse else "ssd_fwd",
    )(*args)


def _flash_kernel(*refs, n_steps, has_ctx):
    if has_ctx:
        (qt_ref, k_ref, vt_ref, kc_ref, vct_ref, o_ref,
         s_ref, p_ref, al_ref, mx_ref, m_ref, acc_ref) = refs
    else:
        qt_ref, k_ref, vt_ref, o_ref, s_ref, p_ref, al_ref, mx_ref, m_ref, acc_ref = refs
    tk = vt_ref.shape[-1]
    dv = vt_ref.shape[-2]
    total = n_steps + (1 if has_ctx else 0)
    qt = qt_ref[0, 0]
    m_ref[...] = jnp.full(m_ref.shape, -jnp.inf, F32)
    acc_ref[...] = jnp.zeros(acc_ref.shape, F32)

    def is_ctx(step):
        return has_ctx and isinstance(step, int) and step == n_steps

    def rows_of(step):
        return kc_ref.shape[2] if is_ctx(step) else tk

    def scores(step, slot):
        if is_ctx(step):
            s = _dot(kc_ref[0, 0], qt)
        else:
            start = step * tk if isinstance(step, int) else pl.multiple_of(step * tk, tk)
            s = _dot(k_ref[0, 0, pl.ds(start, tk), :], qt)
        s_ref[slot, 0:rows_of(step)] = s
        mx_ref[slot] = jnp.max(s, axis=0, keepdims=True)

    def softmax(step, slot):
        rows = rows_of(step)
        s = s_ref[slot, 0:rows]
        m_old = m_ref[...]
        m_new = jnp.maximum(m_old, mx_ref[slot])
        m_ref[...] = m_new
        al_ref[slot] = jnp.exp2(m_old - m_new)
        p_ref[slot, 0:rows] = jnp.exp2(s - m_new).astype(BF16)

    def values(step, slot):
        vt = vct_ref[0, 0, 0] if is_ctx(step) else vt_ref[0, 0, step]
        vt1 = jnp.concatenate([vt, jnp.ones((BF16_ROWS, vt.shape[1]), BF16)], axis=0)
        acc_ref[...] = al_ref[slot] * acc_ref[...] + _dot(vt1, p_ref[slot, 0:rows_of(step)])

    def tick(t, parity):
        static = isinstance(t, int)
        if not static or 0 <= t - 2 * FLASH_LAG < total:
            values(t - 2 * FLASH_LAG, parity)
        if not static or 0 <= t - FLASH_LAG < total:
            softmax(t - FLASH_LAG, parity)
        if not static or t < total:
            scores(t, parity)

    first = 2 * FLASH_LAG
    n_groups = max(n_steps - first, 0) // FLASH_GROUP
    for t in range(first):
        tick(t, t % 2)
    if n_groups > 0:
        def group(i, carry):
            for u in range(FLASH_GROUP):
                tick(first + FLASH_GROUP * i + u, u % 2)
            return carry

        lax.fori_loop(0, n_groups, group, 0)
    for t in range(first + FLASH_GROUP * n_groups, total + 2 * FLASH_LAG):
        tick(t, t % 2)
    o_ref[0] = (acc_ref[0:dv, :] / acc_ref[dv:dv + 1, :]).T.astype(BF16)


def _flash(qt, k, vt, ctx_kv=None):
    bsz, nh, dqk, n = qt.shape
    n_steps, dv, tk = vt.shape[2:]
    tq = min(FLASH_TQ, n)
    has_ctx = ctx_kv is not None
    whole = lambda x: pl.BlockSpec((1, 1) + x.shape[2:], lambda b, h, i: (b, h) + (0,) * (x.ndim - 2))
    in_specs = [pl.BlockSpec((1, 1, dqk, tq), lambda b, h, i: (b, h, 0, i)), whole(k), whole(vt)]
    args = [qt, k, vt]
    if has_ctx:
        kc, vct = ctx_kv
        assert vct.shape[2] == 1 and kc.shape[2] <= tk
        in_specs += [whole(kc), whole(vct)]
        args += [kc, vct]
    return pl.pallas_call(
        functools.partial(_flash_kernel, n_steps=n_steps, has_ctx=has_ctx),
        grid=(bsz, nh, n // tq),
        in_specs=in_specs,
        out_specs=pl.BlockSpec((1, tq, dv), lambda b, h, i: (b, i, h)),
        out_shape=jax.ShapeDtypeStruct((bsz, n, nh * dv), BF16),
        scratch_shapes=[
            pltpu.VMEM((2, tk, tq), F32), pltpu.VMEM((2, tk, tq), BF16),
            pltpu.VMEM((2, 1, tq), F32), pltpu.VMEM((2, 1, tq), F32),
            pltpu.VMEM((1, tq), F32), pltpu.VMEM((dv + BF16_ROWS, tq), F32),
        ],
        compiler_params=_params("arbitrary", "arbitrary", "arbitrary"),
        name="mla_flash",
    )(*args)


def _odd_proj_kernel(h_ref, m_ref, cos_ref, sin_ref, cost_ref, sint_ref, wq_ref, wk_ref, wv_ref, gqt_ref,
                     gk_ref, qt_ref, k_ref, vt_ref):
    u = _modulate(h_ref[0], m_ref[0, 3:4, :], m_ref[0, 4:5, :]).astype(BF16)
    cos = cos_ref[...]
    sin = sin_ref[...]
    cost = cost_ref[...]
    sint = sint_ref[...]
    gqt = gqt_ref[...]
    qd = GQA_HD // 4
    lo = _lane_lo(cos.shape)
    first_quarter = (lax.broadcasted_iota(jnp.int32, cos.shape, 1) % (GQA_HD // 2)) < (GQA_HD // 4)

    def norm_rope(x2, g2):
        sq = x2 * x2
        s_lo = jnp.sum(jnp.where(lo, sq, 0.0), axis=-1, keepdims=True) * (1.0 / GQA_HD)
        s_hi = jnp.sum(jnp.where(lo, 0.0, sq), axis=-1, keepdims=True) * (1.0 / GQA_HD)
        y = x2 * jnp.where(lo, lax.rsqrt(s_lo + EPS), lax.rsqrt(s_hi + EPS)) * g2
        rot = jnp.where(first_quarter, pltpu.roll(y, LANES - GQA_HD // 4, 1), pltpu.roll(y, GQA_HD // 4, 1))
        return y * cos + rot * sin

    q = _dot(u, wq_ref[...])
    k = _dot(u, wk_ref[...])
    v = _dot(u, wv_ref[...])
    for j in range(q.shape[1] // LANES):
        qt = q[:, j * LANES:(j + 1) * LANES].T
        for half in range(2):
            row0 = half * GQA_HD
            y = qt[row0:row0 + GQA_HD]
            y = y * lax.rsqrt(jnp.mean(y * y, axis=0, keepdims=True) + EPS) * gqt
            rot = jnp.concatenate([y[qd:2 * qd], y[0:qd], y[3 * qd:4 * qd], y[2 * qd:3 * qd]], axis=0)
            out = (y * cost + rot * sint) * GQA_PRESCALE
            qt_ref[0, j * LANES + row0:j * LANES + row0 + GQA_HD, :] = out.astype(BF16)
    for j in range(k.shape[1] // LANES):
        sl = slice(j * LANES, (j + 1) * LANES)
        k_ref[0, :, sl] = norm_rope(k[:, sl], gk_ref[...]).astype(BF16)
        vt_ref[0, sl, :] = v[:, sl].T.astype(BF16)


def _odd_proj(h, mod_l, row_of, cos, sin, cos_t, sin_t, w):
    bsz, n, d = h.shape
    tm = _token_tile(n)
    gqt = jnp.broadcast_to(w["gq"].reshape(GQA_HD, 1), (GQA_HD, tm))
    weights = [w["wq"], w["wk"], w["wv"], gqt, w["gk"]]
    tok = lambda width: pl.BlockSpec((1, tm, width), lambda b, i: (b, i, 0))
    nq = GQA_HQ * GQA_HD
    nkv = GQA_HKV * GQA_HD
    return pl.pallas_call(
        _odd_proj_kernel,
        grid=(bsz, n // tm),
        in_specs=[
            tok(d),
            pl.BlockSpec((1, N_MOD, d), lambda b, i: (row_of(b), 0, 0)),
            pl.BlockSpec((tm, LANES), lambda b, i: (i, 0)),
            pl.BlockSpec((tm, LANES), lambda b, i: (i, 0)),
            pl.BlockSpec((GQA_HD, tm), lambda b, i: (0, i)),
            pl.BlockSpec((GQA_HD, tm), lambda b, i: (0, i)),
        ] + [_const_spec(x.shape) for x in weights],
        out_specs=[
            pl.BlockSpec((1, nq, tm), lambda b, i: (b, 0, i)),
            tok(nkv),
            pl.BlockSpec((1, nkv, tm), lambda b, i: (b, 0, i)),
        ],
        out_shape=[
            jax.ShapeDtypeStruct((bsz, nq, n), BF16),
            jax.ShapeDtypeStruct((bsz, n, nkv), BF16),
            jax.ShapeDtypeStruct((bsz, nkv, n), BF16),
        ],
        compiler_params=_params("arbitrary", "arbitrary"),
        name="odd_proj",
    )(h, mod_l, cos, sin, cos_t, sin_t, *weights)


def _band_bias(tq):
    j = np.arange(tq + 2 * WINDOW)[:, None]
    t = (np.arange(2 * GQA_REP * tq) % tq)[None, :]
    return np.where(np.abs(j - WINDOW - t) <= WINDOW, 0.0, -np.inf).astype(np.float32)


def _window_kernel(*refs, nb, local):
    if local:
        (sink_ref, qt_ref, kp_ref, kc_ref, kn_ref, vp_ref, vc_ref, vn_ref, kx_ref, vx_ref, bias_ref,
         o_ref) = refs
    else:
        sink_ref, qt_ref, kx_ref, vx_ref, o_ref = refs
    i = pl.program_id(1)
    tq = qt_ref.shape[2]
    cols = GQA_REP * tq
    top = lax.broadcasted_iota(jnp.int32, (LANES, cols), 0) < GQA_HD
    if local:
        span = tq + 2 * WINDOW
        ninf = jnp.float32(-jnp.inf)
        bias = jnp.concatenate([
            jnp.where(i > 0, bias_ref[0:WINDOW, :], ninf),
            bias_ref[WINDOW:WINDOW + tq, :],
            jnp.where(i < nb - 1, bias_ref[WINDOW + tq:span, :], ninf)], axis=0)
    for gp in range(GQA_HKV // 2):
        fsl = slice(gp * LANES, (gp + 1) * LANES)
        if local:
            k_all = jnp.concatenate(
                [kp_ref[0, :, fsl], kc_ref[0, :, fsl], kn_ref[0, :, fsl], kx_ref[0, :, fsl]], axis=0)
            vt_all = jnp.concatenate(
                [vp_ref[0, fsl, :], vc_ref[0, fsl, :], vn_ref[0, fsl, :], vx_ref[0, fsl, :]], axis=1)
        else:
            k_all = kx_ref[0, :, fsl]
            vt_all = vx_ref[0, fsl, :]
        vt1 = jnp.concatenate([vt_all, jnp.ones((BF16_ROWS, vt_all.shape[1]), BF16)], axis=0)
        qs = jnp.concatenate(
            [qt_ref[0, (gp * GQA_REP + r) * LANES:(gp * GQA_REP + r + 1) * LANES, :] for r in range(GQA_REP)],
            axis=1)
        q2 = jnp.concatenate([jnp.where(top, qs, jnp.zeros_like(qs)), jnp.where(top, jnp.zeros_like(qs), qs)], axis=1)
        sink = jnp.concatenate(
            [jnp.full((1, tq), sink_ref[(2 * gp + half) * GQA_REP + r] * LOG2E, F32)
             for half in range(2) for r in range(GQA_REP)], axis=1)
        s = _dot(k_all, q2)
        if local:
            parts = [s[0:span] + bias, s[span:]]
        else:
            parts = [s]
        m = sink
        for part in parts:
            m = jnp.maximum(m, jnp.max(part, axis=0, keepdims=True))
        p = jnp.concatenate([jnp.exp2(part - m).astype(BF16) for part in parts], axis=0)
        acc = _dot(vt1, p)
        o2 = acc[0:LANES] / (acc[LANES:LANES + 1] + jnp.exp2(sink - m))
        ot = jnp.where(top, o2[:, 0:cols], o2[:, cols:2 * cols])
        for r in range(GQA_REP):
            col = gp * GQA_REP + r
            o_ref[0, :, col * LANES:(col + 1) * LANES] = ot[:, r * tq:(r + 1) * tq].T.astype(BF16)


def _window_attn(qt, kx, vxt, sink, local_kv=None):
    bsz, nq, n = qt.shape
    nkv = kx.shape[2]
    tq = WINDOW
    nb = n // tq
    local = local_kv is not None
    in_specs = [pl.BlockSpec(memory_space=pltpu.SMEM), pl.BlockSpec((1, nq, tq), lambda b, i: (b, 0, i))]
    args = [sink, qt]
    if local:
        k, vt = local_kv
        prev = lambda i: jnp.maximum(i - 1, 0)
        nxt = lambda i: jnp.minimum(i + 1, nb - 1)
        in_specs += [pl.BlockSpec((1, tq, nkv), lambda b, i: (b, prev(i), 0)),
                     pl.BlockSpec((1, tq, nkv), lambda b, i: (b, i, 0)),
                     pl.BlockSpec((1, tq, nkv), lambda b, i: (b, nxt(i), 0)),
                     pl.BlockSpec((1, nkv, tq), lambda b, i: (b, 0, prev(i))),
                     pl.BlockSpec((1, nkv, tq), lambda b, i: (b, 0, i)),
                     pl.BlockSpec((1, nkv, tq), lambda b, i: (b, 0, nxt(i)))]
        args += [k, k, k, vt, vt, vt]
    in_specs += [pl.BlockSpec((1,) + kx.shape[1:], lambda b, i: (b, 0, 0)),
                 pl.BlockSpec((1,) + vxt.shape[1:], lambda b, i: (b, 0, 0))]
    args += [kx, vxt]
    if local:
        bias = jnp.asarray(_band_bias(tq))
        in_specs.append(_const_spec(bias.shape))
        args.append(bias)
    return pl.pallas_call(
        functools.partial(_window_kernel, nb=nb, local=local),
        grid=(bsz, nb),
        in_specs=in_specs,
        out_specs=pl.BlockSpec((1, tq, nq), lambda b, i: (b, i, 0)),
        out_shape=jax.ShapeDtypeStruct((bsz, n, nq), BF16),
        compiler_params=_params("arbitrary", "arbitrary"),
        name="window_attn" if local else "sink_attn",
    )(*args)


_ROT_SRC = np.concatenate([np.arange(16, 32), np.arange(0, 16), np.arange(48, 64), np.arange(32, 48)])
_ROT_SIGN = np.concatenate([-np.ones(16), np.ones(16), -np.ones(16), np.ones(16)]).astype(np.float32)


def _rope_tables(n):
    quarter = MLA_DR // 4
    freqs = ROPE_BASE ** (-jnp.arange(quarter, dtype=F32) / quarter)
    t = jnp.arange(n)
    row = (t // GRID_W).astype(F32)
    col = (t % GRID_W).astype(F32)
    ang_r = row[:, None] * freqs
    ang_c = col[:, None] * freqs
    cos = jnp.concatenate([jnp.cos(ang_r), jnp.cos(ang_r), jnp.cos(ang_c), jnp.cos(ang_c)], axis=-1)
    sin = jnp.concatenate([jnp.sin(ang_r), jnp.sin(ang_r), jnp.sin(ang_c), jnp.sin(ang_c)], axis=-1)
    return cos, sin


def _ffn_weights(w_in, w_out):
    d = w_in.shape[0]
    wg = w_in[:, :FFN_DIM].reshape(d, N_FFN_CHUNKS, FFN_CHUNK).transpose(1, 0, 2).astype(BF16)
    wu = w_in[:, FFN_DIM:].reshape(d, N_FFN_CHUNKS, FFN_CHUNK).transpose(1, 0, 2).astype(BF16)
    return wg, wu, w_out.astype(BF16)


def _pad_lanes(row, offset=0):
    out = jnp.zeros((1, LANES), F32)
    return out.at[0, offset:offset + row.shape[0]].set(row.astype(F32))


def _even_weights(w_in, conv_w, conv_b, dt_bias, a_log, d_skip, ssd_norm, qa_norm, w_qb, kv_norm,
                  w_kvb, qn_g, qr_g, kn_g, kr_g, w_out):
    d = w_in.shape[0]
    sign = jnp.asarray(_ROT_SIGN)
    w_dt = w_in[:, _E1:_E2]
    w_kr = w_in[:, _E4:]
    wm = jnp.concatenate([w_dt, jnp.zeros((d, LANES - 2 * SSD_HEADS), F32), w_kr, w_kr[:, _ROT_SRC] * sign], axis=1)
    wq = w_qb.reshape(MLA_Q_RANK, MLA_HEADS, MLA_DN + MLA_DR)
    wqr = wq[:, :, MLA_DN:].reshape(MLA_Q_RANK, MLA_HEADS * MLA_DR)
    wkv = w_kvb.reshape(MLA_KV_RANK, MLA_HEADS, MLA_DN + MLA_DV)
    row = lambda g: g.reshape(1, -1).astype(F32)
    return {
        "wz": w_in[:, :_E0].astype(BF16),
        "wx": w_in[:, _E0:_E1].astype(BF16),
        "wm": wm.astype(BF16),
        "wqa": w_in[:, _E2:_E3].astype(BF16),
        "wkv": w_in[:, _E3:_E4].astype(BF16),
        "wqn": wq[:, :, :MLA_DN].reshape(MLA_Q_RANK, MLA_HEADS * MLA_DN).astype(BF16),
        "wqr": wqr.astype(BF16),
        "wkn": wkv[:, :, :MLA_DN].reshape(MLA_KV_RANK, MLA_HEADS * MLA_DN).astype(BF16),
        "wv": wkv[:, :, MLA_DN:].reshape(MLA_KV_RANK, MLA_HEADS * MLA_DV).astype(BF16),
        "gqa": row(qa_norm), "gkv": row(kv_norm), "gqn": row(qn_g), "gkn": row(kn_g),
        "gqr": row(qr_g),
        "gkr": row(jnp.concatenate([kr_g, kr_g[_ROT_SRC]])),
        "conv_w": conv_w.astype(F32), "conv_b": row(conv_b),
        "dt_bias": _pad_lanes(dt_bias.reshape(-1)),
        "a_fwd": _pad_lanes(-jnp.exp(a_log[0].astype(F32)), 0),
        "a_rev": _pad_lanes(-jnp.exp(a_log[1].astype(F32)), SSD_HEADS),
        "d_skip": row(jnp.repeat(d_skip, SSD_P)), "ssd_norm": row(ssd_norm),
        "wo_ssd": w_out[:SSD_DI].astype(BF16), "wo_mla": w_out[SSD_DI:].astype(BF16),
    }


def _gqa_col_perm():
    perm = np.zeros(GQA_HQ * GQA_HD, np.int32)
    for gp in range(GQA_HKV // 2):
        for r in range(GQA_REP):
            for half in range(2):
                head = (2 * gp + half) * GQA_REP + r
                new = ((gp * GQA_REP + r) * 2 + half) * GQA_HD
                perm[new:new + GQA_HD] = np.arange(head * GQA_HD, (head + 1) * GQA_HD)
    return perm


_GQA_PERM = _gqa_col_perm()


def _odd_weights(w_in, q_g, k_g, sink, w_out):
    nq = GQA_HQ * GQA_HD
    nkv = GQA_HKV * GQA_HD
    tile2 = lambda g: jnp.concatenate([g, g]).reshape(1, LANES).astype(F32)
    return {
        "wq": w_in[:, :nq][:, _GQA_PERM].astype(BF16),
        "wk": w_in[:, nq:nq + nkv].astype(BF16),
        "wv": w_in[:, nq + nkv:].astype(BF16),
        "gq": q_g.astype(F32), "gk": tile2(k_g),
        "sink": sink.astype(F32),
        "wo": w_out[_GQA_PERM, :].astype(BF16),
    }


def kernel(x, c, ctx, c_ctx, w_mod, b_mod, w_ff1_in, w_ff1_out, w_ff2_in, w_ff2_out, w_in_e, conv_w, conv_b, dt_bias, a_log, d_skip, ssd_norm, mla_qa_norm, w_qb, mla_kv_norm, w_kvb, mla_qn_norm, mla_qr_norm, mla_kn_norm, mla_kr_norm, w_out_e, w_in_o, gqa_q_norm, gqa_k_norm, sink, w_out_o):
    bsz, n, d = x.shape
    lc = ctx.shape[1]
    depth = w_mod.shape[0]
    assert d == D_MODEL and bsz + 1 <= MOD_ROWS and n % TOKEN_TILE == 0 and lc % CHUNK == 0

    cond = jnp.zeros((MOD_ROWS, d), F32).at[:bsz].set(c).at[bsz].set(c_ctx)
    mod = _mod_vectors(cond, w_mod, b_mod).reshape(depth, MOD_ROWS, N_MOD, d)
    lat_row = lambda b: b
    ctx_row = lambda b: bsz

    cos64, sin64 = _rope_tables(n)
    tab_lat = jnp.concatenate([cos64, sin64], axis=-1)
    tab_ctx = jnp.concatenate([jnp.ones((lc, MLA_DR), F32), jnp.zeros((lc, MLA_DR), F32)], axis=-1)
    sign = jnp.asarray(_ROT_SIGN)
    cos_lat = jnp.concatenate([cos64, cos64], axis=-1)
    sin_lat = jnp.concatenate([sin64 * sign, sin64 * sign], axis=-1)
    cos_ctx = jnp.ones((lc, LANES), F32)
    sin_ctx = jnp.zeros((lc, LANES), F32)
    cos_t_lat = cos64.T
    sin_t_lat = (sin64 * sign).T
    cos_t_ctx = jnp.ones((GQA_HD, lc), F32)
    sin_t_ctx = jnp.zeros((GQA_HD, lc), F32)

    h, hc = x, ctx
    for l in range(depth):
        need_ctx = l < depth - 1
        mod_l = mod[l]
        ff1 = _ffn_weights(w_ff1_in[l], w_ff1_out[l])
        ff2 = _ffn_weights(w_ff2_in[l], w_ff2_out[l])
        h = _ffn(h, mod_l, lat_row, *ff1, k0=0)
        hc = _ffn(hc, mod_l, ctx_row, *ff1, k0=0)
        if l % 2 == 0:
            e = l // 2
            w = _even_weights(w_in_e[e], conv_w[e], conv_b[e], dt_bias[e], a_log[e], d_skip[e], ssd_norm[e],
                              mla_qa_norm[e], w_qb[e], mla_kv_norm[e], w_kvb[e], mla_qn_norm[e],
                              mla_qr_norm[e], mla_kn_norm[e], mla_kr_norm[e], w_out_e[e])
            zc, xbcc, dtc, qc, kc, vc = _even_proj(hc, mod_l, ctx_row, tab_ctx, cos_t_ctx, sin_t_ctx, w)
            z, xbc, dtm, q, k, v = _even_proj(h, mod_l, lat_row, tab_lat, cos_t_lat, sin_t_lat, w)
            s0 = jnp.zeros((bsz, SSD_HEADS // 2, SSD_N, LANES), F32)
            yfc, s_fc, xac = _ssd(xbcc, dtc, s0, w, reverse=False)
            ysc, s_bc = _ssd(xac, dtc, s0, w, reverse=True, merge=(yfc, zc))
            yf, _, xa = _ssd(xbc, dtm, s_fc, w, reverse=False)
            ys, _ = _ssd(xa, dtm, s_bc, w, reverse=True, merge=(yf, z))
            o = _flash(q, k, v, ctx_kv=(kc, vc))
            mix = ((ys, w["wo_ssd"]), (o, w["wo_mla"]))
            if need_ctx:
                mix_c = ((ysc, w["wo_ssd"]), (_flash(qc, kc, vc), w["wo_mla"]))
        else:
            o_ = l // 2
            w = _odd_weights(w_in_o[o_], gqa_q_norm[o_], gqa_k_norm[o_], sink[o_], w_out_o[o_])
            qc, kc, vc = _odd_proj(hc, mod_l, ctx_row, cos_ctx, sin_ctx, cos_t_ctx, sin_t_ctx, w)
            q, k, v = _odd_proj(h, mod_l, lat_row, cos_lat, sin_lat, cos_t_lat, sin_t_lat, w)
            mix = ((_window_attn(q, kc, vc, w["sink"], local_kv=(k, v)), w["wo"]),)
            if need_ctx:
                mix_c = ((_window_attn(qc, kc, vc, w["sink"]), w["wo"]),)
        h = _ffn(h, mod_l, lat_row, *ff2, k0=6, mix=mix)
        if need_ctx:
            hc = _ffn(hc, mod_l, ctx_row, *ff2, k0=6, mix=mix_c)
    return h
```

```python
import functools
import math

import jax
import jax.numpy as jnp
import numpy as np
from jax import lax
from jax.experimental import pallas as pl
from jax.experimental.pallas import tpu as pltpu

F32 = jnp.float32
BF16 = jnp.bfloat16
HIGHEST = lax.Precision.HIGHEST

D_MODEL = 1024
GRID_W = 64
N_MOD = 9
FFN_DIM = 2816
SSD_P = 64
SSD_HEADS = 16
SSD_DI = 1024
SSD_G = 2
SSD_N = 128
D_CONV = 5
CONV_CH = SSD_DI + 2 * SSD_G * SSD_N
CHUNK = 128
MLA_DN = 128
MLA_DR = 64
MLA_DV = 128
MLA_HEADS = 8
MLA_Q_RANK = 384
MLA_KV_RANK = 256
GQA_HD = 64
GQA_HQ = 16
GQA_HKV = 4
GQA_REP = 4
WINDOW = 128
ROPE_BASE = 10000.0
EPS = 1e-6
_E0 = SSD_DI
_E1 = _E0 + CONV_CH
_E2 = _E1 + 2 * SSD_HEADS
_E3 = _E2 + MLA_Q_RANK
_E4 = _E3 + MLA_KV_RANK
E_IN = _E4 + MLA_DR
MLA_SCALE = (MLA_DN + MLA_DR) ** -0.5
GQA_SCALE = GQA_HD ** -0.5
LOG2E = math.log2(math.e)
Q_PRESCALE = MLA_SCALE * LOG2E
GQA_PRESCALE = GQA_SCALE * LOG2E

LANES = 128
SUBLANES = 8
BF16_ROWS = 16
VMEM_LIMIT_BYTES = 56 * 1024 * 1024
MOD_ROWS = 16
FFN_CHUNK = 256
N_FFN_CHUNKS = FFN_DIM // FFN_CHUNK
TOKEN_TILE = 512
FLASH_TQ = 1024
FLASH_LAG = 2
FLASH_GROUP = 6
HALO = SUBLANES


def _params(*sem):
    return pltpu.CompilerParams(dimension_semantics=sem, vmem_limit_bytes=VMEM_LIMIT_BYTES)


def _const_spec(shape):
    nd = len(shape)
    return pl.BlockSpec(shape, lambda *_: (0,) * nd, pipeline_mode=pl.Buffered(1))


def _dot(a, b):
    return jnp.dot(a, b, preferred_element_type=F32)


def _dot_nt(a, b):
    return lax.dot_general(a, b, (((1,), (1,)), ((), ())), preferred_element_type=F32)


def _silu(x):
    return x * jax.nn.sigmoid(x)


def _rms(x):
    return x * lax.rsqrt(jnp.mean(x * x, axis=-1, keepdims=True) + EPS)


def _modulate(h, shift, scale):
    return _rms(h) * (1.0 + scale) + shift


def _lane_lo(shape):
    return (lax.broadcasted_iota(jnp.int32, shape, len(shape) - 1) % LANES) < (LANES // 2)


def _mod_kernel(s_ref, w_ref, b_ref, o_ref):
    s = _silu(s_ref[...])
    o_ref[0] = jnp.dot(s, w_ref[0], preferred_element_type=F32, precision=HIGHEST) + b_ref[0]


def _mod_vectors(cond, w_mod, b_mod):
    depth, d, nm = w_mod.shape
    tn = 1024
    return pl.pallas_call(
        _mod_kernel,
        grid=(depth, nm // tn),
        in_specs=[
            pl.BlockSpec((MOD_ROWS, d), lambda l, j: (0, 0)),
            pl.BlockSpec((1, d, tn), lambda l, j: (l, 0, j)),
            pl.BlockSpec((1, 1, tn), lambda l, j: (l, 0, j)),
        ],
        out_specs=pl.BlockSpec((1, MOD_ROWS, tn), lambda l, j: (l, 0, j)),
        out_shape=jax.ShapeDtypeStruct((depth, MOD_ROWS, nm), F32),
        compiler_params=_params("arbitrary", "arbitrary"),
        name="mod_vectors",
    )(cond, w_mod, b_mod.reshape(depth, 1, nm))


def _ffn_kernel(*refs, k0, n_mix):
    h_ref, m_ref = refs[0], refs[1]
    mix_a = refs[2:2 + n_mix]
    mix_w = refs[2 + n_mix:2 + 2 * n_mix]
    wg_ref, wu_ref, wo_ref, o_ref, a_ref = refs[2 + 2 * n_mix:]
    h = h_ref[0]
    if n_mix:
        y = _dot(mix_a[0][0], mix_w[0][...])
        for ar, wr in zip(mix_a[1:], mix_w[1:]):
            y = y + _dot(ar[0], wr[...])
        h = h + m_ref[0, 5:6, :] * y
    xm = _modulate(h, m_ref[0, k0:k0 + 1, :], m_ref[0, k0 + 1:k0 + 2, :]).astype(BF16)
    for j in range(N_FFN_CHUNKS):
        g = _dot(xm, wg_ref[j])
        u = _dot(xm, wu_ref[j])
        a_ref[:, j * FFN_CHUNK:(j + 1) * FFN_CHUNK] = (_silu(g) * u).astype(BF16)
    y = _dot(a_ref[...], wo_ref[...])
    o_ref[0] = h + (0.5 * m_ref[0, k0 + 2:k0 + 3, :]) * y


def _token_tile(n):
    return min(TOKEN_TILE, n)


def _ffn(h, mod_l, row_of, wg, wu, wo, k0, mix=()):
    bsz, n, d = h.shape
    tm = _token_tile(n)
    acts = [a for a, _ in mix]
    mix_w = [x for _, x in mix]
    return pl.pallas_call(
        functools.partial(_ffn_kernel, k0=k0, n_mix=len(mix)),
        grid=(bsz, n // tm),
        in_specs=[
            pl.BlockSpec((1, tm, d), lambda b, i: (b, i, 0)),
            pl.BlockSpec((1, N_MOD, d), lambda b, i: (row_of(b), 0, 0)),
        ] + [pl.BlockSpec((1, tm, a.shape[-1]), lambda b, i: (b, i, 0)) for a in acts]
        + [_const_spec(x.shape) for x in mix_w + [wg, wu, wo]],
        out_specs=pl.BlockSpec((1, tm, d), lambda b, i: (b, i, 0)),
        out_shape=jax.ShapeDtypeStruct(h.shape, F32),
        scratch_shapes=[pltpu.VMEM((tm, FFN_DIM), BF16)],
        compiler_params=_params("arbitrary", "arbitrary"),
        name="ffn_mix" if mix else "ffn",
    )(h, mod_l, *acts, *mix_w, wg, wu, wo)


def _norm_rope_pair(t2, g2, tab):
    lo = _lane_lo(t2.shape)
    ss = jnp.sum(jnp.where(lo, t2 * t2, 0.0), axis=-1, keepdims=True) * (1.0 / MLA_DR)
    y = t2 * lax.rsqrt(ss + EPS) * g2 * tab
    return y + pltpu.roll(y, LANES // 2, 1)


def _even_proj_kernel(h_ref, m_ref, tab_ref, cost_ref, sint_ref, wz_ref, wx_ref, wm_ref, wqa_ref, wkv_ref,
                      wqn_ref, wqr_ref, wkn_ref, wv_ref, gqa_ref, gkv_ref, gqnt_ref, gqrt_ref, gkn_ref,
                      gkr_ref, z_ref, xbc_ref, dt_ref, qt_ref, k_ref, vt_ref):
    u = _modulate(h_ref[0], m_ref[0, 3:4, :], m_ref[0, 4:5, :]).astype(BF16)
    tab = tab_ref[...]
    cost = cost_ref[...]
    sint = sint_ref[...]
    qd = MLA_DR // 4
    z_ref[0] = _dot(u, wz_ref[...])
    xbc_ref[0] = _dot(u, wx_ref[...])
    misc = _dot(u, wm_ref[...])
    dt_ref[0] = misc[:, :LANES]
    kr = _norm_rope_pair(misc[:, LANES:], gkr_ref[...], tab)[:, :MLA_DR].astype(BF16)
    qa = (_rms(_dot(u, wqa_ref[...])) * gqa_ref[...]).astype(BF16)
    qn = _dot(qa, wqn_ref[...])
    qr = _dot(qa, wqr_ref[...])
    ckv = (_rms(_dot(u, wkv_ref[...])) * gkv_ref[...]).astype(BF16)
    kn = _dot(ckv, wkn_ref[...])
    v = _dot(ckv, wv_ref[...])
    for hd in range(MLA_HEADS):
        sl = slice(hd * LANES, (hd + 1) * LANES)
        qnt = qn[:, sl].T
        qnt = qnt * lax.rsqrt(jnp.mean(qnt * qnt, axis=0, keepdims=True) + EPS) * gqnt_ref[...]
        qt_ref[0, hd, 0:MLA_DN, :] = (qnt * Q_PRESCALE).astype(BF16)
        k_ref[0, hd, :, 0:MLA_DN] = (_rms(kn[:, sl]) * gkn_ref[...]).astype(BF16)
        k_ref[0, hd, :, MLA_DN:MLA_DN + MLA_DR] = kr
        vt_ref[0, hd, 0] = v[:, sl].T.astype(BF16)
    for pair in range(MLA_HEADS // 2):
        qrt = qr[:, pair * LANES:(pair + 1) * LANES].T
        for half in range(2):
            y = qrt[half * MLA_DR:(half + 1) * MLA_DR]
            y = y * lax.rsqrt(jnp.mean(y * y, axis=0, keepdims=True) + EPS) * gqrt_ref[...]
            rot = jnp.concatenate([y[qd:2 * qd], y[0:qd], y[3 * qd:4 * qd], y[2 * qd:3 * qd]], axis=0)
            out = (y * cost + rot * sint) * Q_PRESCALE
            qt_ref[0, 2 * pair + half, MLA_DN:MLA_DN + MLA_DR, :] = out.astype(BF16)


def _even_proj(h, mod_l, row_of, tab, cos_t, sin_t, w):
    bsz, n, d = h.shape
    tm = _token_tile(n)
    gqnt = jnp.broadcast_to(w["gqn"].reshape(MLA_DN, 1), (MLA_DN, tm))
    gqrt = jnp.broadcast_to(w["gqr"].reshape(MLA_DR, 1), (MLA_DR, tm))
    weights = [w["wz"], w["wx"], w["wm"], w["wqa"], w["wkv"], w["wqn"], w["wqr"], w["wkn"], w["wv"],
               w["gqa"], w["gkv"], gqnt, gqrt, w["gkn"], w["gkr"]]
    dqk = MLA_DN + MLA_DR
    tok = lambda width: pl.BlockSpec((1, tm, width), lambda b, i: (b, i, 0))
    return pl.pallas_call(
        _even_proj_kernel,
        grid=(bsz, n // tm),
        in_specs=[
            tok(d),
            pl.BlockSpec((1, N_MOD, d), lambda b, i: (row_of(b), 0, 0)),
            pl.BlockSpec((tm, LANES), lambda b, i: (i, 0)),
            pl.BlockSpec((MLA_DR, tm), lambda b, i: (0, i)),
            pl.BlockSpec((MLA_DR, tm), lambda b, i: (0, i)),
        ] + [_const_spec(x.shape) for x in weights],
        out_specs=[
            tok(SSD_DI), tok(CONV_CH), tok(LANES),
            pl.BlockSpec((1, MLA_HEADS, dqk, tm), lambda b, i: (b, 0, 0, i)),
            pl.BlockSpec((1, MLA_HEADS, tm, dqk), lambda b, i: (b, 0, i, 0)),
            pl.BlockSpec((1, MLA_HEADS, 1, MLA_DV, tm), lambda b, i: (b, 0, i, 0, 0)),
        ],
        out_shape=[
            jax.ShapeDtypeStruct((bsz, n, SSD_DI), F32),
            jax.ShapeDtypeStruct((bsz, n, CONV_CH), F32),
            jax.ShapeDtypeStruct((bsz, n, LANES), F32),
            jax.ShapeDtypeStruct((bsz, MLA_HEADS, dqk, n), BF16),
            jax.ShapeDtypeStruct((bsz, MLA_HEADS, n, dqk), BF16),
            jax.ShapeDtypeStruct((bsz, MLA_HEADS, n // tm, MLA_DV, tm), BF16),
        ],
        compiler_params=_params("arbitrary", "arbitrary"),
        name="even_proj",
    )(h, mod_l, tab, cos_t, sin_t, *weights)


def _ssd_kernel(*refs, nc, reverse, lane0):
    if reverse:
        (xa_ref, dt_ref, dtn_ref, dtb_ref, a_ref, h0_ref, yf_ref, z_ref, dsk_ref, nrm_ref,
         y_ref, st_ref, dec_ref, gat_ref) = refs
    else:
        (xm_ref, xp_ref, xn_ref, dt_ref, dtn_ref, cw_ref, cb_ref, sh_ref, dtb_ref, a_ref, h0_ref,
         y_ref, st_ref, xa_ref, dec_ref) = refs
    c = pl.program_id(1)
    ri = lax.broadcasted_iota(jnp.int32, (CHUNK, CHUNK), 0)
    ci = lax.broadcasted_iota(jnp.int32, (CHUNK, CHUNK), 1)
    tri = (ci >= ri) if reverse else (ri >= ci)

    def store_decay_terms(raw, slot):
        xr = raw + dtb_ref[...]
        dt = jnp.maximum(xr, 0.0) + jnp.log1p(jnp.exp(-jnp.abs(xr)))
        acs = jnp.dot(tri.astype(F32), dt * a_ref[...], preferred_element_type=F32, precision=HIGHEST)
        dec_ref[slot, 0] = acs
        dec_ref[slot, 1] = acs.T
        dec_ref[slot, 2] = dt.T

    @pl.when(c == 0)
    def _():
        st_ref[...] = h0_ref[...]
        store_decay_terms(dt_ref[0], 0)

    cur = c % 2
    acs = dec_ref[cur, 0]
    acs_t = dec_ref[cur, 1]
    dt_t = dec_ref[cur, 2]
    store_decay_terms(dtn_ref[0], 1 - cur)

    if reverse:
        xbc = xa_ref[0]
    else:
        xmid = xm_ref[0]
        ext = jnp.concatenate(
            [jnp.where(c == 0, 0.0, xp_ref[0]), xmid, jnp.where(c == nc - 1, 0.0, xn_ref[0])], axis=0)
        shifted = _dot(sh_ref[...], ext.astype(BF16))
        mid = D_CONV // 2
        acc = cb_ref[...] + cw_ref[mid:mid + 1, :] * xmid
        for idx, k in enumerate([k for k in range(D_CONV) if k != mid]):
            acc = acc + cw_ref[k:k + 1, :] * shifted[idx * CHUNK:(idx + 1) * CHUNK]
        xbc = _silu(acc)
        xa_ref[0] = xbc
    xs = xbc[:, :SSD_DI]

    edge = 0 if reverse else CHUNK - 1
    lo = _lane_lo((CHUNK, LANES))
    lo_row = _lane_lo((1, LANES))

    cb_g, bt_g, c_g = [], [], []
    for g in range(SSD_G):
        bg = xbc[:, SSD_DI + g * SSD_N:SSD_DI + (g + 1) * SSD_N]
        cg = xbc[:, SSD_DI + SSD_G * SSD_N + g * SSD_N:SSD_DI + SSD_G * SSD_N + (g + 1) * SSD_N]
        cgb = cg.astype(BF16)
        cb_g.append(_dot_nt(cgb, bg.astype(BF16)))
        bt_g.append(bg.T)
        c_g.append(cgb)

    def head_terms(j, g):
        col = acs[:, j:j + 1]
        row = acs_t[j:j + 1, :]
        dtrow = dt_t[j:j + 1, :]
        lmat = jnp.exp(jnp.where(tri, col - row, -jnp.inf))
        gmat = (cb_g[g] * lmat * dtrow).astype(BF16)
        tot = acs_t[j:j + 1, edge:edge + 1]
        bw = (bt_g[g] * (jnp.exp(tot - row) * dtrow)).astype(BF16)
        return gmat, bw, jnp.exp(col), jnp.exp(tot)

    for k in range(SSD_HEADS // 2):
        g = (2 * k) // (SSD_HEADS // SSD_G)
        xpair = xs[:, k * LANES:(k + 1) * LANES]
        rhs = jnp.concatenate([jnp.where(lo, xpair, 0.0), jnp.where(lo, 0.0, xpair)], axis=0).astype(BF16)
        ga, bwa, ea, da = head_terms(lane0 + 2 * k, g)
        gb, bwb, eb, db = head_terms(lane0 + 2 * k + 1, g)
        s_in = st_ref[0, k]
        y = (_dot(jnp.concatenate([ga, gb], axis=1), rhs)
             + _dot(c_g[g], s_in.astype(BF16)) * jnp.where(lo, ea, eb))
        st_ref[0, k] = s_in * jnp.where(lo_row, da, db) + _dot(jnp.concatenate([bwa, bwb], axis=1), rhs)
        sl = slice(k * LANES, (k + 1) * LANES)
        if reverse:
            zt = z_ref[0, :, sl]
            gat_ref[:, sl] = (yf_ref[0, :, sl] + y + dsk_ref[:, sl] * xpair) * _silu(zt)
        else:
            y_ref[0, :, sl] = y

    if reverse:
        gw = SSD_DI // SSD_G
        for g in range(SSD_G):
            seg = gat_ref[:, g * gw:(g + 1) * gw]
            y_ref[0, :, g * gw:(g + 1) * gw] = (_rms(seg) * nrm_ref[:, g * gw:(g + 1) * gw]).astype(BF16)


def _conv_shift_matrices():
    taps = [k for k in range(D_CONV) if k != D_CONV // 2]
    out = np.zeros((len(taps) * CHUNK, CHUNK + 2 * HALO), np.float32)
    for idx, k in enumerate(taps):
        out[idx * CHUNK + np.arange(CHUNK), np.arange(CHUNK) + HALO - D_CONV // 2 + k] = 1.0
    return out


_CONV_SHIFTS = _conv_shift_matrices()


def _ssd(xbc, dtm, h0, w, reverse, merge=None):
    bsz, n, _ = xbc.shape
    nc = n // CHUNK
    per = CHUNK // HALO
    nhb = n // HALO
    cidx = (lambda c: nc - 1 - c) if reverse else (lambda c: c)
    tok = lambda width: pl.BlockSpec((1, CHUNK, width), lambda b, c: (b, cidx(c), 0))
    state_spec = pl.BlockSpec((1, SSD_HEADS // 2, SSD_N, LANES), lambda b, c: (b, 0, 0, 0))
    a_row = w["a_rev" if reverse else "a_fwd"]
    dt_next = pl.BlockSpec((1, CHUNK, LANES), lambda b, c: (b, cidx(jnp.minimum(c + 1, nc - 1)), 0))
    dec_scratch = pltpu.VMEM((2, 3, CHUNK, LANES), F32)
    y_spec = [tok(SSD_DI), state_spec]
    y_shape = [jax.ShapeDtypeStruct((bsz, n, SSD_DI), BF16 if reverse else F32), jax.ShapeDtypeStruct(h0.shape, F32)]
    if reverse:
        y_fwd, z = merge
        in_specs = [tok(CONV_CH), tok(LANES), dt_next, _const_spec(w["dt_bias"].shape), _const_spec(a_row.shape),
                    state_spec, tok(SSD_DI), tok(SSD_DI), _const_spec(w["d_skip"].shape),
                    _const_spec(w["ssd_norm"].shape)]
        args = [xbc, dtm, dtm, w["dt_bias"], a_row, h0, y_fwd, z, w["d_skip"], w["ssd_norm"]]
        scratch = [dec_scratch, pltpu.VMEM((CHUNK, SSD_DI), F32)]
    else:
        in_specs = [
            tok(CONV_CH),
            pl.BlockSpec((1, HALO, CONV_CH), lambda b, c: (b, jnp.maximum(c * per - 1, 0), 0)),
            pl.BlockSpec((1, HALO, CONV_CH), lambda b, c: (b, jnp.minimum((c + 1) * per, nhb - 1), 0)),
            tok(LANES),
            dt_next,
            _const_spec(w["conv_w"].shape),
            _const_spec(w["conv_b"].shape),
            _const_spec(_CONV_SHIFTS.shape),
            _const_spec(w["dt_bias"].shape),
            _const_spec(a_row.shape),
            state_spec,
        ]
        args = [xbc, xbc, xbc, dtm, dtm, w["conv_w"], w["conv_b"], jnp.asarray(_CONV_SHIFTS, BF16), w["dt_bias"],
                a_row, h0]
        scratch = [dec_scratch]
        y_spec.append(tok(CONV_CH))
        y_shape.append(jax.ShapeDtypeStruct((bsz, n, CONV_CH), F32))
    return pl.pallas_call(
        functools.partial(_ssd_kernel, nc=nc, reverse=reverse, lane0=SSD_HEADS if reverse else 0),
        grid=(bsz, nc),
        in_specs=in_specs,
        out_specs=y_spec,
        out_shape=y_shape,
        scratch_shapes=scratch,
        compiler_params=_params("arbitrary", "arbitrary"),
        name="ssd_rev" if reverse else "ssd_fwd",
    )(*args)


def _flash_kernel(*refs, n_steps, has_ctx):
    if has_ctx:
        (qt_ref, k_ref, vt_ref, kc_ref, vct_ref, o_ref,
         s_ref, p_ref, al_ref, mx_ref, m_ref, acc_ref) = refs
    else:
        qt_ref, k_ref, vt_ref, o_ref, s_ref, p_ref, al_ref, mx_ref, m_ref, acc_ref = refs
    tk = vt_ref.shape[-1]
    dv = vt_ref.shape[-2]
    total = n_steps + (1 if has_ctx else 0)
    qt = qt_ref[0, 0]
    m_ref[...] = jnp.full(m_ref.shape, -jnp.inf, F32)
    acc_ref[...] = jnp.zeros(acc_ref.shape, F32)

    def is_ctx(step):
        return has_ctx and isinstance(step, int) and step == n_steps

    def rows_of(step):
        return kc_ref.shape[2] if is_ctx(step) else tk

    def scores(step, slot):
        if is_ctx(step):
            s = _dot(kc_ref[0, 0], qt)
        else:
            start = step * tk if isinstance(step, int) else pl.multiple_of(step * tk, tk)
            s = _dot(k_ref[0, 0, pl.ds(start, tk), :], qt)
        s_ref[slot, 0:rows_of(step)] = s
        mx_ref[slot] = jnp.max(s, axis=0, keepdims=True)

    def softmax(step, slot):
        rows = rows_of(step)
        s = s_ref[slot, 0:rows]
        m_old = m_ref[...]
        m_new = jnp.maximum(m_old, mx_ref[slot])
        m_ref[...] = m_new
        al_ref[slot] = jnp.exp2(m_old - m_new)
        p_ref[slot, 0:rows] = jnp.exp2(s - m_new).astype(BF16)

    def values(step, slot):
        vt = vct_ref[0, 0, 0] if is_ctx(step) else vt_ref[0, 0, step]
        vt1 = jnp.concatenate([vt, jnp.ones((BF16_ROWS, vt.shape[1]), BF16)], axis=0)
        acc_ref[...] = al_ref[slot] * acc_ref[...] + _dot(vt1, p_ref[slot, 0:rows_of(step)])

    def tick(t, parity):
        static = isinstance(t, int)
        if not static or 0 <= t - 2 * FLASH_LAG < total:
            values(t - 2 * FLASH_LAG, parity)
        if not static or 0 <= t - FLASH_LAG < total:
            softmax(t - FLASH_LAG, parity)
        if not static or t < total:
            scores(t, parity)

    first = 2 * FLASH_LAG
    n_groups = max(n_steps - first, 0) // FLASH_GROUP
    for t in range(first):
        tick(t, t % 2)
    if n_groups > 0:
        def group(i, carry):
            for u in range(FLASH_GROUP):
                tick(first + FLASH_GROUP * i + u, u % 2)
            return carry

        lax.fori_loop(0, n_groups, group, 0)
    for t in range(first + FLASH_GROUP * n_groups, total + 2 * FLASH_LAG):
        tick(t, t % 2)
    o_ref[0] = (acc_ref[0:dv, :] / acc_ref[dv:dv + 1, :]).T.astype(BF16)


def _flash(qt, k, vt, ctx_kv=None):
    bsz, nh, dqk, n = qt.shape
    n_steps, dv, tk = vt.shape[2:]
    tq = min(FLASH_TQ, n)
    has_ctx = ctx_kv is not None
    whole = lambda x: pl.BlockSpec((1, 1) + x.shape[2:], lambda b, h, i: (b, h) + (0,) * (x.ndim - 2))
    in_specs = [pl.BlockSpec((1, 1, dqk, tq), lambda b, h, i: (b, h, 0, i)), whole(k), whole(vt)]
    args = [qt, k, vt]
    if has_ctx:
        kc, vct = ctx_kv
        assert vct.shape[2] == 1 and kc.shape[2] <= tk
        in_specs += [whole(kc), whole(vct)]
        args += [kc, vct]
    return pl.pallas_call(
        functools.partial(_flash_kernel, n_steps=n_steps, has_ctx=has_ctx),
        grid=(bsz, nh, n // tq),
        in_specs=in_specs,
        out_specs=pl.BlockSpec((1, tq, dv), lambda b, h, i: (b, i, h)),
        out_shape=jax.ShapeDtypeStruct((bsz, n, nh * dv), BF16),
        scratch_shapes=[
            pltpu.VMEM((2, tk, tq), F32), pltpu.VMEM((2, tk, tq), BF16),
            pltpu.VMEM((2, 1, tq), F32), pltpu.VMEM((2, 1, tq), F32),
            pltpu.VMEM((1, tq), F32), pltpu.VMEM((dv + BF16_ROWS, tq), F32),
        ],
        compiler_params=_params("arbitrary", "arbitrary", "arbitrary"),
        name="mla_flash",
    )(*args)


def _odd_proj_kernel(h_ref, m_ref, cos_ref, sin_ref, cost_ref, sint_ref, wq_ref, wk_ref, wv_ref, gqt_ref,
                     gk_ref, qt_ref, k_ref, vt_ref):
    u = _modulate(h_ref[0], m_ref[0, 3:4, :], m_ref[0, 4:5, :]).astype(BF16)
    cos = cos_ref[...]
    sin = sin_ref[...]
    cost = cost_ref[...]
    sint = sint_ref[...]
    gqt = gqt_ref[...]
    qd = GQA_HD // 4
    lo = _lane_lo(cos.shape)
    first_quarter = (lax.broadcasted_iota(jnp.int32, cos.shape, 1) % (GQA_HD // 2)) < (GQA_HD // 4)

    def norm_rope(x2, g2):
        sq = x2 * x2
        s_lo = jnp.sum(jnp.where(lo, sq, 0.0), axis=-1, keepdims=True) * (1.0 / GQA_HD)
        s_hi = jnp.sum(jnp.where(lo, 0.0, sq), axis=-1, keepdims=True) * (1.0 / GQA_HD)
        y = x2 * jnp.where(lo, lax.rsqrt(s_lo + EPS), lax.rsqrt(s_hi + EPS)) * g2
        rot = jnp.where(first_quarter, pltpu.roll(y, LANES - GQA_HD // 4, 1), pltpu.roll(y, GQA_HD // 4, 1))
        return y * cos + rot * sin

    q = _dot(u, wq_ref[...])
    k = _dot(u, wk_ref[...])
    v = _dot(u, wv_ref[...])
    for j in range(q.shape[1] // LANES):
        qt = q[:, j * LANES:(j + 1) * LANES].T
        for half in range(2):
            row0 = half * GQA_HD
            y = qt[row0:row0 + GQA_HD]
            y = y * lax.rsqrt(jnp.mean(y * y, axis=0, keepdims=True) + EPS) * gqt
            rot = jnp.concatenate([y[qd:2 * qd], y[0:qd], y[3 * qd:4 * qd], y[2 * qd:3 * qd]], axis=0)
            out = (y * cost + rot * sint) * GQA_PRESCALE
            qt_ref[0, j * LANES + row0:j * LANES + row0 + GQA_HD, :] = out.astype(BF16)
    for j in range(k.shape[1] // LANES):
        sl = slice(j * LANES, (j + 1) * LANES)
        k_ref[0, :, sl] = norm_rope(k[:, sl], gk_ref[...]).astype(BF16)
        vt_ref[0, sl, :] = v[:, sl].T.astype(BF16)


def _odd_proj(h, mod_l, row_of, cos, sin, cos_t, sin_t, w):
    bsz, n, d = h.shape
    tm = _token_tile(n)
    gqt = jnp.broadcast_to(w["gq"].reshape(GQA_HD, 1), (GQA_HD, tm))
    weights = [w["wq"], w["wk"], w["wv"], gqt, w["gk"]]
    tok = lambda width: pl.BlockSpec((1, tm, width), lambda b, i: (b, i, 0))
    nq = GQA_HQ * GQA_HD
    nkv = GQA_HKV * GQA_HD
    return pl.pallas_call(
        _odd_proj_kernel,
        grid=(bsz, n // tm),
        in_specs=[
            tok(d),
            pl.BlockSpec((1, N_MOD, d), lambda b, i: (row_of(b), 0, 0)),
            pl.BlockSpec((tm, LANES), lambda b, i: (i, 0)),
            pl.BlockSpec((tm, LANES), lambda b, i: (i, 0)),
            pl.BlockSpec((GQA_HD, tm), lambda b, i: (0, i)),
            pl.BlockSpec((GQA_HD, tm), lambda b, i: (0, i)),
        ] + [_const_spec(x.shape) for x in weights],
        out_specs=[
            pl.BlockSpec((1, nq, tm), lambda b, i: (b, 0, i)),
            tok(nkv),
            pl.BlockSpec((1, nkv, tm), lambda b, i: (b, 0, i)),
        ],
        out_shape=[
            jax.ShapeDtypeStruct((bsz, nq, n), BF16),
            jax.ShapeDtypeStruct((bsz, n, nkv), BF16),
            jax.ShapeDtypeStruct((bsz, nkv, n), BF16),
        ],
        compiler_params=_params("arbitrary", "arbitrary"),
        name="odd_proj",
    )(h, mod_l, cos, sin, cos_t, sin_t, *weights)


def _band_bias(tq):
    j = np.arange(tq + 2 * WINDOW)[:, None]
    t = (np.arange(2 * GQA_REP * tq) % tq)[None, :]
    return np.where(np.abs(j - WINDOW - t) <= WINDOW, 0.0, -np.inf).astype(np.float32)


def _window_kernel(*refs, nb, local):
    if local:
        (sink_ref, qt_ref, kp_ref, kc_ref, kn_ref, vp_ref, vc_ref, vn_ref, kx_ref, vx_ref, bias_ref,
         o_ref) = refs
    else:
        sink_ref, qt_ref, kx_ref, vx_ref, o_ref = refs
    i = pl.program_id(1)
    tq = qt_ref.shape[2]
    cols = GQA_REP * tq
    top = lax.broadcasted_iota(jnp.int32, (LANES, cols), 0) < GQA_HD
    if local:
        span = tq + 2 * WINDOW
        ninf = jnp.float32(-jnp.inf)
        bias = jnp.concatenate([
            jnp.where(i > 0, bias_ref[0:WINDOW, :], ninf),
            bias_ref[WINDOW:WINDOW + tq, :],
            jnp.where(i < nb - 1, bias_ref[WINDOW + tq:span, :], ninf)], axis=0)
    for gp in range(GQA_HKV // 2):
        fsl = slice(gp * LANES, (gp + 1) * LANES)
        if local:
            k_all = jnp.concatenate(
                [kp_ref[0, :, fsl], kc_ref[0, :, fsl], kn_ref[0, :, fsl], kx_ref[0, :, fsl]], axis=0)
            vt_all = jnp.concatenate(
                [vp_ref[0, fsl, :], vc_ref[0, fsl, :], vn_ref[0, fsl, :], vx_ref[0, fsl, :]], axis=1)
        else:
            k_all = kx_ref[0, :, fsl]
            vt_all = vx_ref[0, fsl, :]
        vt1 = jnp.concatenate([vt_all, jnp.ones((BF16_ROWS, vt_all.shape[1]), BF16)], axis=0)
        qs = jnp.concatenate(
            [qt_ref[0, (gp * GQA_REP + r) * LANES:(gp * GQA_REP + r + 1) * LANES, :] for r in range(GQA_REP)],
            axis=1)
        q2 = jnp.concatenate([jnp.where(top, qs, jnp.zeros_like(qs)), jnp.where(top, jnp.zeros_like(qs), qs)], axis=1)
        sink = jnp.concatenate(
            [jnp.full((1, tq), sink_ref[(2 * gp + half) * GQA_REP + r] * LOG2E, F32)
             for half in range(2) for r in range(GQA_REP)], axis=1)
        s = _dot(k_all, q2)
        if local:
            parts = [s[0:span] + bias, s[span:]]
        else:
            parts = [s]
        m = sink
        for part in parts:
            m = jnp.maximum(m, jnp.max(part, axis=0, keepdims=True))
        p = jnp.concatenate([jnp.exp2(part - m).astype(BF16) for part in parts], axis=0)
        acc = _dot(vt1, p)
        o2 = acc[0:LANES] / (acc[LANES:LANES + 1] + jnp.exp2(sink - m))
        ot = jnp.where(top, o2[:, 0:cols], o2[:, cols:2 * cols])
        for r in range(GQA_REP):
            col = gp * GQA_REP + r
            o_ref[0, :, col * LANES:(col + 1) * LANES] = ot[:, r * tq:(r + 1) * tq].T.astype(BF16)


def _window_attn(qt, kx, vxt, sink, local_kv=None):
    bsz, nq, n = qt.shape
    nkv = kx.shape[2]
    tq = WINDOW
    nb = n // tq
    local = local_kv is not None
    in_specs = [pl.BlockSpec(memory_space=pltpu.SMEM), pl.BlockSpec((1, nq, tq), lambda b, i: (b, 0, i))]
    args = [sink, qt]
    if local:
        k, vt = local_kv
        prev = lambda i: jnp.maximum(i - 1, 0)
        nxt = lambda i: jnp.minimum(i + 1, nb - 1)
        in_specs += [pl.BlockSpec((1, tq, nkv), lambda b, i: (b, prev(i), 0)),
                     pl.BlockSpec((1, tq, nkv), lambda b, i: (b, i, 0)),
                     pl.BlockSpec((1, tq, nkv), lambda b, i: (b, nxt(i), 0)),
                     pl.BlockSpec((1, nkv, tq), lambda b, i: (b, 0, prev(i))),
                     pl.BlockSpec((1, nkv, tq), lambda b, i: (b, 0, i)),
                     pl.BlockSpec((1, nkv, tq), lambda b, i: (b, 0, nxt(i)))]
        args += [k, k, k, vt, vt, vt]
    in_specs += [pl.BlockSpec((1,) + kx.shape[1:], lambda b, i: (b, 0, 0)),
                 pl.BlockSpec((1,) + vxt.shape[1:], lambda b, i: (b, 0, 0))]
    args += [kx, vxt]
    if local:
        bias = jnp.asarray(_band_bias(tq))
        in_specs.append(_const_spec(bias.shape))
        args.append(bias)
    return pl.pallas_call(
        functools.partial(_window_kernel, nb=nb, local=local),
        grid=(bsz, nb),
        in_specs=in_specs,
        out_specs=pl.BlockSpec((1, tq, nq), lambda b, i: (b, i, 0)),
        out_shape=jax.ShapeDtypeStruct((bsz, n, nq), BF16),
        compiler_params=_params("arbitrary", "arbitrary"),
        name="window_attn" if local else "sink_attn",
    )(*args)


_ROT_SRC = np.concatenate([np.arange(16, 32), np.arange(0, 16), np.arange(48, 64), np.arange(32, 48)])
_ROT_SIGN = np.concatenate([-np.ones(16), np.ones(16), -np.ones(16), np.ones(16)]).astype(np.float32)


def _rope_tables(n):
    quarter = MLA_DR // 4
    freqs = ROPE_BASE ** (-jnp.arange(quarter, dtype=F32) / quarter)
    t = jnp.arange(n)
    row = (t // GRID_W).astype(F32)
    col = (t % GRID_W).astype(F32)
    ang_r = row[:, None] * freqs
    ang_c = col[:, None] * freqs
    cos = jnp.concatenate([jnp.cos(ang_r), jnp.cos(ang_r), jnp.cos(ang_c), jnp.cos(ang_c)], axis=-1)
    sin = jnp.concatenate([jnp.sin(ang_r), jnp.sin(ang_r), jnp.sin(ang_c), jnp.sin(ang_c)], axis=-1)
    return cos, sin


def _ffn_weights(w_in, w_out):
    d = w_in.shape[0]
    wg = w_in[:, :FFN_DIM].reshape(d, N_FFN_CHUNKS, FFN_CHUNK).transpose(1, 0, 2).astype(BF16)
    wu = w_in[:, FFN_DIM:].reshape(d, N_FFN_CHUNKS, FFN_CHUNK).transpose(1, 0, 2).astype(BF16)
    return wg, wu, w_out.astype(BF16)


def _pad_lanes(row, offset=0):
    out = jnp.zeros((1, LANES), F32)
    return out.at[0, offset:offset + row.shape[0]].set(row.astype(F32))


def _even_weights(w_in, conv_w, conv_b, dt_bias, a_log, d_skip, ssd_norm, qa_norm, w_qb, kv_norm,
                  w_kvb, qn_g, qr_g, kn_g, kr_g, w_out):
    d = w_in.shape[0]
    sign = jnp.asarray(_ROT_SIGN)
    w_dt = w_in[:, _E1:_E2]
    w_kr = w_in[:, _E4:]
    wm = jnp.concatenate([w_dt, jnp.zeros((d, LANES - 2 * SSD_HEADS), F32), w_kr, w_kr[:, _ROT_SRC] * sign], axis=1)
    wq = w_qb.reshape(MLA_Q_RANK, MLA_HEADS, MLA_DN + MLA_DR)
    wqr = wq[:, :, MLA_DN:].reshape(MLA_Q_RANK, MLA_HEADS * MLA_DR)
    wkv = w_kvb.reshape(MLA_KV_RANK, MLA_HEADS, MLA_DN + MLA_DV)
    row = lambda g: g.reshape(1, -1).astype(F32)
    return {
        "wz": w_in[:, :_E0].astype(BF16),
        "wx": w_in[:, _E0:_E1].astype(BF16),
        "wm": wm.astype(BF16),
        "wqa": w_in[:, _E2:_E3].astype(BF16),
        "wkv": w_in[:, _E3:_E4].astype(BF16),
        "wqn": wq[:, :, :MLA_DN].reshape(MLA_Q_RANK, MLA_HEADS * MLA_DN).astype(BF16),
        "wqr": wqr.astype(BF16),
        "wkn": wkv[:, :, :MLA_DN].reshape(MLA_KV_RANK, MLA_HEADS * MLA_DN).astype(BF16),
        "wv": wkv[:, :, MLA_DN:].reshape(MLA_KV_RANK, MLA_HEADS * MLA_DV).astype(BF16),
        "gqa": row(qa_norm), "gkv": row(kv_norm), "gqn": row(qn_g), "gkn": row(kn_g),
        "gqr": row(qr_g),
        "gkr": row(jnp.concatenate([kr_g, kr_g[_ROT_SRC]])),
        "conv_w": conv_w.astype(F32), "conv_b": row(conv_b),
        "dt_bias": _pad_lanes(dt_bias.reshape(-1)),
        "a_fwd": _pad_lanes(-jnp.exp(a_log[0].astype(F32)), 0),
        "a_rev": _pad_lanes(-jnp.exp(a_log[1].astype(F32)), SSD_HEADS),
        "d_skip": row(jnp.repeat(d_skip, SSD_P)), "ssd_norm": row(ssd_norm),
        "wo_ssd": w_out[:SSD_DI].astype(BF16), "wo_mla": w_out[SSD_DI:].astype(BF16),
    }


def _gqa_col_perm():
    perm = np.zeros(GQA_HQ * GQA_HD, np.int32)
    for gp in range(GQA_HKV // 2):
        for r in range(GQA_REP):
            for half in range(2):
                head = (2 * gp + half) * GQA_REP + r
                new = ((gp * GQA_REP + r) * 2 + half) * GQA_HD
                perm[new:new + GQA_HD] = np.arange(head * GQA_HD, (head + 1) * GQA_HD)
    return perm


_GQA_PERM = _gqa_col_perm()


def _odd_weights(w_in, q_g, k_g, sink, w_out):
    nq = GQA_HQ * GQA_HD
    nkv = GQA_HKV * GQA_HD
    tile2 = lambda g: jnp.concatenate([g, g]).reshape(1, LANES).astype(F32)
    return {
        "wq": w_in[:, :nq][:, _GQA_PERM].astype(BF16),
        "wk": w_in[:, nq:nq + nkv].astype(BF16),
        "wv": w_in[:, nq + nkv:].astype(BF16),
        "gq": q_g.astype(F32), "gk": tile2(k_g),
        "sink": sink.astype(F32),
        "wo": w_out[_GQA_PERM, :].astype(BF16),
    }


def kernel(x, c, ctx, c_ctx, w_mod, b_mod, w_ff1_in, w_ff1_out, w_ff2_in, w_ff2_out, w_in_e, conv_w, conv_b, dt_bias, a_log, d_skip, ssd_norm, mla_qa_norm, w_qb, mla_kv_norm, w_kvb, mla_qn_norm, mla_qr_norm, mla_kn_norm, mla_kr_norm, w_out_e, w_in_o, gqa_q_norm, gqa_k_norm, sink, w_out_o):
    bsz, n, d = x.shape
    lc = ctx.shape[1]
    depth = w_mod.shape[0]
    assert d == D_MODEL and bsz + 1 <= MOD_ROWS and n % TOKEN_TILE == 0 and lc % CHUNK == 0

    cond = jnp.zeros((MOD_ROWS, d), F32).at[:bsz].set(c).at[bsz].set(c_ctx)
    mod = _mod_vectors(cond, w_mod, b_mod).reshape(depth, MOD_ROWS, N_MOD, d)
    lat_row = lambda b: b
    ctx_row = lambda b: bsz

    cos64, sin64 = _rope_tables(n)
    tab_lat = jnp.concatenate([cos64, sin64], axis=-1)
    tab_ctx = jnp.concatenate([jnp.ones((lc, MLA_DR), F32), jnp.zeros((lc, MLA_DR), F32)], axis=-1)
    sign = jnp.asarray(_ROT_SIGN)
    cos_lat = jnp.concatenate([cos64, cos64], axis=-1)
    sin_lat = jnp.concatenate([sin64 * sign, sin64 * sign], axis=-1)
    cos_ctx = jnp.ones((lc, LANES), F32)
    sin_ctx = jnp.zeros((lc, LANES), F32)
    cos_t_lat = cos64.T
    sin_t_lat = (sin64 * sign).T
    cos_t_ctx = jnp.ones((GQA_HD, lc), F32)
    sin_t_ctx = jnp.zeros((GQA_HD, lc), F32)

    h, hc = x, ctx
    for l in range(depth):
        need_ctx = l < depth - 1
        mod_l = mod[l]
        ff1 = _ffn_weights(w_ff1_in[l], w_ff1_out[l])
        ff2 = _ffn_weights(w_ff2_in[l], w_ff2_out[l])
        h = _ffn(h, mod_l, lat_row, *ff1, k0=0)
        hc = _ffn(hc, mod_l, ctx_row, *ff1, k0=0)
        if l % 2 == 0:
            e = l // 2
            w = _even_weights(w_in_e[e], conv_w[e], conv_b[e], dt_bias[e], a_log[e], d_skip[e], ssd_norm[e],
                              mla_qa_norm[e], w_qb[e], mla_kv_norm[e], w_kvb[e], mla_qn_norm[e],
                              mla_qr_norm[e], mla_kn_norm[e], mla_kr_norm[e], w_out_e[e])
            zc, xbcc, dtc, qc, kc, vc = _even_proj(hc, mod_l, ctx_row, tab_ctx, cos_t_ctx, sin_t_ctx, w)
            z, xbc, dtm, q, k, v = _even_proj(h, mod_l, lat_row, tab_lat, cos_t_lat, sin_t_lat, w)
            s0 = jnp.zeros((bsz, SSD_HEADS // 2, SSD_N, LANES), F32)
            yfc, s_fc, xac = _ssd(xbcc, dtc, s0, w, reverse=False)
            ysc, s_bc = _ssd(xac, dtc, s0, w, reverse=True, merge=(yfc, zc))
            yf, _, xa = _ssd(xbc, dtm, s_fc, w, reverse=False)
            ys, _ = _ssd(xa, dtm, s_bc, w, reverse=True, merge=(yf, z))
            o = _flash(q, k, v, ctx_kv=(kc, vc))
            mix = ((ys, w["wo_ssd"]), (o, w["wo_mla"]))
            if need_ctx:
                mix_c = ((ysc, w["wo_ssd"]), (_flash(qc, kc, vc), w["wo_mla"]))
        else:
            o_ = l // 2
            w = _odd_weights(w_in_o[o_], gqa_q_norm[o_], gqa_k_norm[o_], sink[o_], w_out_o[o_])
            qc, kc, vc = _odd_proj(hc, mod_l, ctx_row, cos_ctx, sin_ctx, cos_t_ctx, sin_t_ctx, w)
            q, k, v = _odd_proj(h, mod_l, lat_row, cos_lat, sin_lat, cos_t_lat, sin_t_lat, w)
            mix = ((_window_attn(q, kc, vc, w["sink"], local_kv=(k, v)), w["wo"]),)
            if need_ctx:
                mix_c = ((_window_attn(qc, kc, vc, w["sink"]), w["wo"]),)
        h = _ffn(h, mod_l, lat_row, *ff2, k0=6, mix=mix)
        if need_ctx:
            hc = _ffn(hc, mod_l, ctx_row, *ff2, k0=6, mix=mix_c)
    return h
```

```python
import functools
import math

import jax
import jax.numpy as jnp
import numpy as np
from jax import lax
from jax.experimental import pallas as pl
from jax.experimental.pallas import tpu as pltpu

F32 = jnp.float32
BF16 = jnp.bfloat16
HIGHEST = lax.Precision.HIGHEST

D_MODEL = 1024
GRID_W = 64
N_MOD = 9
FFN_DIM = 2816
SSD_P = 64
SSD_HEADS = 16
SSD_DI = 1024
SSD_G = 2
SSD_N = 128
D_CONV = 5
CONV_CH = SSD_DI + 2 * SSD_G * SSD_N
CHUNK = 128
MLA_DN = 128
MLA_DR = 64
MLA_DV = 128
MLA_HEADS = 8
MLA_Q_RANK = 384
MLA_KV_RANK = 256
GQA_HD = 64
GQA_HQ = 16
GQA_HKV = 4
GQA_REP = 4
WINDOW = 128
ROPE_BASE = 10000.0
EPS = 1e-6
_E0 = SSD_DI
_E1 = _E0 + CONV_CH
_E2 = _E1 + 2 * SSD_HEADS
_E3 = _E2 + MLA_Q_RANK
_E4 = _E3 + MLA_KV_RANK
E_IN = _E4 + MLA_DR
MLA_SCALE = (MLA_DN + MLA_DR) ** -0.5
GQA_SCALE = GQA_HD ** -0.5
LOG2E = math.log2(math.e)
Q_PRESCALE = MLA_SCALE * LOG2E
GQA_PRESCALE = GQA_SCALE * LOG2E

LANES = 128
SUBLANES = 8
BF16_ROWS = 16
VMEM_LIMIT_BYTES = 56 * 1024 * 1024
MOD_ROWS = 16
FFN_CHUNK = 256
N_FFN_CHUNKS = FFN_DIM // FFN_CHUNK
TOKEN_TILE = 512
FLASH_TQ = 1024
FLASH_LAG = 2
FLASH_GROUP = 6
HALO = SUBLANES
SSD_SUBS = 2


def _params(*sem):
    return pltpu.CompilerParams(dimension_semantics=sem, vmem_limit_bytes=VMEM_LIMIT_BYTES)


def _const_spec(shape):
    nd = len(shape)
    return pl.BlockSpec(shape, lambda *_: (0,) * nd, pipeline_mode=pl.Buffered(1))


def _dot(a, b):
    return jnp.dot(a, b, preferred_element_type=F32)


def _dot_nt(a, b):
    return lax.dot_general(a, b, (((1,), (1,)), ((), ())), preferred_element_type=F32)


def _silu(x):
    return x * jax.nn.sigmoid(x)


def _rms(x):
    return x * lax.rsqrt(jnp.mean(x * x, axis=-1, keepdims=True) + EPS)


def _modulate(h, shift, scale):
    return _rms(h) * (1.0 + scale) + shift


def _lane_lo(shape):
    return (lax.broadcasted_iota(jnp.int32, shape, len(shape) - 1) % LANES) < (LANES // 2)


def _mod_kernel(s_ref, w_ref, b_ref, o_ref):
    s = _silu(s_ref[...])
    o_ref[0] = jnp.dot(s, w_ref[0], preferred_element_type=F32, precision=HIGHEST) + b_ref[0]


def _mod_vectors(cond, w_mod, b_mod):
    depth, d, nm = w_mod.shape
    tn = 1024
    return pl.pallas_call(
        _mod_kernel,
        grid=(depth, nm // tn),
        in_specs=[
            pl.BlockSpec((MOD_ROWS, d), lambda l, j: (0, 0)),
            pl.BlockSpec((1, d, tn), lambda l, j: (l, 0, j)),
            pl.BlockSpec((1, 1, tn), lambda l, j: (l, 0, j)),
        ],
        out_specs=pl.BlockSpec((1, MOD_ROWS, tn), lambda l, j: (l, 0, j)),
        out_shape=jax.ShapeDtypeStruct((depth, MOD_ROWS, nm), F32),
        compiler_params=_params("arbitrary", "arbitrary"),
        name="mod_vectors",
    )(cond, w_mod, b_mod.reshape(depth, 1, nm))


def _ffn_kernel(*refs, k0, n_mix):
    h_ref, m_ref = refs[0], refs[1]
    mix_a = refs[2:2 + n_mix]
    mix_w = refs[2 + n_mix:2 + 2 * n_mix]
    wg_ref, wu_ref, wo_ref, o_ref, a_ref = refs[2 + 2 * n_mix:]
    h = h_ref[0]
    if n_mix:
        y = _dot(mix_a[0][0], mix_w[0][...])
        for ar, wr in zip(mix_a[1:], mix_w[1:]):
            y = y + _dot(ar[0], wr[...])
        h = h + m_ref[0, 5:6, :] * y
    xm = _modulate(h, m_ref[0, k0:k0 + 1, :], m_ref[0, k0 + 1:k0 + 2, :]).astype(BF16)
    for j in range(N_FFN_CHUNKS):
        g = _dot(xm, wg_ref[j])
        u = _dot(xm, wu_ref[j])
        a_ref[:, j * FFN_CHUNK:(j + 1) * FFN_CHUNK] = (_silu(g) * u).astype(BF16)
    y = _dot(a_ref[...], wo_ref[...])
    o_ref[0] = h + (0.5 * m_ref[0, k0 + 2:k0 + 3, :]) * y


def _token_tile(n):
    return min(TOKEN_TILE, n)


def _ffn(h, mod_l, row_of, wg, wu, wo, k0, mix=()):
    bsz, n, d = h.shape
    tm = _token_tile(n)
    acts = [a for a, _ in mix]
    mix_w = [x for _, x in mix]
    return pl.pallas_call(
        functools.partial(_ffn_kernel, k0=k0, n_mix=len(mix)),
        grid=(bsz, n // tm),
        in_specs=[
            pl.BlockSpec((1, tm, d), lambda b, i: (b, i, 0)),
            pl.BlockSpec((1, N_MOD, d), lambda b, i: (row_of(b), 0, 0)),
        ] + [pl.BlockSpec((1, tm, a.shape[-1]), lambda b, i: (b, i, 0)) for a in acts]
        + [_const_spec(x.shape) for x in mix_w + [wg, wu, wo]],
        out_specs=pl.BlockSpec((1, tm, d), lambda b, i: (b, i, 0)),
        out_shape=jax.ShapeDtypeStruct(h.shape, F32),
        scratch_shapes=[pltpu.VMEM((tm, FFN_DIM), BF16)],
        compiler_params=_params("arbitrary", "arbitrary"),
        name="ffn_mix" if mix else "ffn",
    )(h, mod_l, *acts, *mix_w, wg, wu, wo)


def _norm_rope_pair(t2, g2, tab):
    lo = _lane_lo(t2.shape)
    ss = jnp.sum(jnp.where(lo, t2 * t2, 0.0), axis=-1, keepdims=True) * (1.0 / MLA_DR)
    y = t2 * lax.rsqrt(ss + EPS) * g2 * tab
    return y + pltpu.roll(y, LANES // 2, 1)


def _even_proj_kernel(h_ref, m_ref, tab_ref, cost_ref, sint_ref, wz_ref, wx_ref, wm_ref, wqa_ref, wkv_ref,
                      wqn_ref, wqr_ref, wkn_ref, wv_ref, gqa_ref, gkv_ref, gqnt_ref, gqrt_ref, gkn_ref,
                      gkr_ref, z_ref, xbc_ref, dt_ref, qt_ref, k_ref, vt_ref):
    u = _modulate(h_ref[0], m_ref[0, 3:4, :], m_ref[0, 4:5, :]).astype(BF16)
    tab = tab_ref[...]
    cost = cost_ref[...]
    sint = sint_ref[...]
    qd = MLA_DR // 4
    z_ref[0] = _dot(u, wz_ref[...])
    xbc_ref[0] = _dot(u, wx_ref[...])
    misc = _dot(u, wm_ref[...])
    dt_ref[0] = misc[:, :LANES]
    kr = _norm_rope_pair(misc[:, LANES:], gkr_ref[...], tab)[:, :MLA_DR].astype(BF16)
    qa = (_rms(_dot(u, wqa_ref[...])) * gqa_ref[...]).astype(BF16)
    qn = _dot(qa, wqn_ref[...])
    qr = _dot(qa, wqr_ref[...])
    ckv = (_rms(_dot(u, wkv_ref[...])) * gkv_ref[...]).astype(BF16)
    kn = _dot(ckv, wkn_ref[...])
    v = _dot(ckv, wv_ref[...])
    for hd in range(MLA_HEADS):
        sl = slice(hd * LANES, (hd + 1) * LANES)
        qnt = qn[:, sl].T
        qnt = qnt * lax.rsqrt(jnp.mean(qnt * qnt, axis=0, keepdims=True) + EPS) * gqnt_ref[...]
        qt_ref[0, hd, 0:MLA_DN, :] = (qnt * Q_PRESCALE).astype(BF16)
        k_ref[0, hd, :, 0:MLA_DN] = (_rms(kn[:, sl]) * gkn_ref[...]).astype(BF16)
        k_ref[0, hd, :, MLA_DN:MLA_DN + MLA_DR] = kr
        vt_ref[0, hd, 0] = v[:, sl].T.astype(BF16)
    for pair in range(MLA_HEADS // 2):
        qrt = qr[:, pair * LANES:(pair + 1) * LANES].T
        for half in range(2):
            y = qrt[half * MLA_DR:(half + 1) * MLA_DR]
            y = y * lax.rsqrt(jnp.mean(y * y, axis=0, keepdims=True) + EPS) * gqrt_ref[...]
            rot = jnp.concatenate([y[qd:2 * qd], y[0:qd], y[3 * qd:4 * qd], y[2 * qd:3 * qd]], axis=0)
            out = (y * cost + rot * sint) * Q_PRESCALE
            qt_ref[0, 2 * pair + half, MLA_DN:MLA_DN + MLA_DR, :] = out.astype(BF16)


def _even_proj(h, mod_l, row_of, tab, cos_t, sin_t, w):
    bsz, n, d = h.shape
    tm = _token_tile(n)
    gqnt = jnp.broadcast_to(w["gqn"].reshape(MLA_DN, 1), (MLA_DN, tm))
    gqrt = jnp.broadcast_to(w["gqr"].reshape(MLA_DR, 1), (MLA_DR, tm))
    weights = [w["wz"], w["wx"], w["wm"], w["wqa"], w["wkv"], w["wqn"], w["wqr"], w["wkn"], w["wv"],
               w["gqa"], w["gkv"], gqnt, gqrt, w["gkn"], w["gkr"]]
    dqk = MLA_DN + MLA_DR
    tok = lambda width: pl.BlockSpec((1, tm, width), lambda b, i: (b, i, 0))
    return pl.pallas_call(
        _even_proj_kernel,
        grid=(bsz, n // tm),
        in_specs=[
            tok(d),
            pl.BlockSpec((1, N_MOD, d), lambda b, i: (row_of(b), 0, 0)),
            pl.BlockSpec((tm, LANES), lambda b, i: (i, 0)),
            pl.BlockSpec((MLA_DR, tm), lambda b, i: (0, i)),
            pl.BlockSpec((MLA_DR, tm), lambda b, i: (0, i)),
        ] + [_const_spec(x.shape) for x in weights],
        out_specs=[
            tok(SSD_DI), tok(CONV_CH), tok(LANES),
            pl.BlockSpec((1, MLA_HEADS, dqk, tm), lambda b, i: (b, 0, 0, i)),
            pl.BlockSpec((1, MLA_HEADS, tm, dqk), lambda b, i: (b, 0, i, 0)),
            pl.BlockSpec((1, MLA_HEADS, 1, MLA_DV, tm), lambda b, i: (b, 0, i, 0, 0)),
        ],
        out_shape=[
            jax.ShapeDtypeStruct((bsz, n, SSD_DI), F32),
            jax.ShapeDtypeStruct((bsz, n, CONV_CH), F32),
            jax.ShapeDtypeStruct((bsz, n, LANES), F32),
            jax.ShapeDtypeStruct((bsz, MLA_HEADS, dqk, n), BF16),
            jax.ShapeDtypeStruct((bsz, MLA_HEADS, n, dqk), BF16),
            jax.ShapeDtypeStruct((bsz, MLA_HEADS, n // tm, MLA_DV, tm), BF16),
        ],
        compiler_params=_params("arbitrary", "arbitrary"),
        name="even_proj",
    )(h, mod_l, tab, cos_t, sin_t, *weights)


def _ssd_kernel(*refs, nb, reverse, lane0):
    if reverse:
        (xa_ref, dt_ref, dtn_ref, dtb_ref, a_ref, h0_ref, yf_ref, z_ref, dsk_ref, nrm_ref,
         y_ref, st_ref, dec_ref, gat_ref) = refs
    else:
        (xm_ref, xp_ref, xn_ref, dt_ref, dtn_ref, cw_ref, cb_ref, sh_ref, dtb_ref, a_ref, h0_ref,
         y_ref, st_ref, xa_ref, dec_ref) = refs
    c = pl.program_id(1)
    ri = lax.broadcasted_iota(jnp.int32, (CHUNK, CHUNK), 0)
    ci = lax.broadcasted_iota(jnp.int32, (CHUNK, CHUNK), 1)
    tri = (ci >= ri) if reverse else (ri >= ci)

    def store_decay_terms(raw_ref, slot):
        for sub in range(SSD_SUBS):
            xr = raw_ref[0, sub * CHUNK:(sub + 1) * CHUNK, :] + dtb_ref[...]
            dt = jnp.maximum(xr, 0.0) + jnp.log1p(jnp.exp(-jnp.abs(xr)))
            acs = jnp.dot(tri.astype(F32), dt * a_ref[...], preferred_element_type=F32, precision=HIGHEST)
            dec_ref[slot, sub, 0] = acs
            dec_ref[slot, sub, 1] = acs.T
            dec_ref[slot, sub, 2] = dt.T

    @pl.when(c == 0)
    def _():
        st_ref[...] = h0_ref[...]
        store_decay_terms(dt_ref, 0)

    cur = c % 2
    terms = [(dec_ref[cur, sub, 0], dec_ref[cur, sub, 1], dec_ref[cur, sub, 2]) for sub in range(SSD_SUBS)]
    store_decay_terms(dtn_ref, 1 - cur)

    edge = 0 if reverse else CHUNK - 1
    lo = _lane_lo((CHUNK, LANES))
    lo_row = _lane_lo((1, LANES))
    mid = D_CONV // 2

    for sub in (range(SSD_SUBS - 1, -1, -1) if reverse else range(SSD_SUBS)):
        rows = slice(sub * CHUNK, (sub + 1) * CHUNK)
        acs, acs_t, dt_t = terms[sub]
        if reverse:
            xbc = xa_ref[0, rows, :]
        else:
            xmid = xm_ref[0, rows, :]
            if sub == 0:
                before = jnp.where(c == 0, 0.0, xp_ref[0])
            else:
                before = xm_ref[0, sub * CHUNK - HALO:sub * CHUNK, :]
            if sub == SSD_SUBS - 1:
                after = jnp.where(c == nb - 1, 0.0, xn_ref[0])
            else:
                after = xm_ref[0, (sub + 1) * CHUNK:(sub + 1) * CHUNK + HALO, :]
            ext = jnp.concatenate([before, xmid, after], axis=0)
            shifted = _dot(sh_ref[...], ext.astype(BF16))
            acc = cb_ref[...] + cw_ref[mid:mid + 1, :] * xmid
            for idx, k in enumerate([k for k in range(D_CONV) if k != mid]):
                acc = acc + cw_ref[k:k + 1, :] * shifted[idx * CHUNK:(idx + 1) * CHUNK]
            xbc = _silu(acc)
            xa_ref[0, rows, :] = xbc
        xs = xbc[:, :SSD_DI]

        cb_g, bt_g, c_g = [], [], []
        for g in range(SSD_G):
            bg = xbc[:, SSD_DI + g * SSD_N:SSD_DI + (g + 1) * SSD_N]
            cg = xbc[:, SSD_DI + SSD_G * SSD_N + g * SSD_N:SSD_DI + SSD_G * SSD_N + (g + 1) * SSD_N]
            cgb = cg.astype(BF16)
            cb_g.append(_dot_nt(cgb, bg.astype(BF16)))
            bt_g.append(bg.T)
            c_g.append(cgb)

        def head_terms(j, g):
            col = acs[:, j:j + 1]
            row = acs_t[j:j + 1, :]
            dtrow = dt_t[j:j + 1, :]
            lmat = jnp.exp(jnp.where(tri, col - row, -jnp.inf))
            gmat = (cb_g[g] * lmat * dtrow).astype(BF16)
            tot = acs_t[j:j + 1, edge:edge + 1]
            bw = (bt_g[g] * (jnp.exp(tot - row) * dtrow)).astype(BF16)
            return gmat, bw, jnp.exp(col), jnp.exp(tot)

        for k in range(SSD_HEADS // 2):
            g = (2 * k) // (SSD_HEADS // SSD_G)
            xpair = xs[:, k * LANES:(k + 1) * LANES]
            rhs = jnp.concatenate([jnp.where(lo, xpair, 0.0), jnp.where(lo, 0.0, xpair)], axis=0).astype(BF16)
            ga, bwa, ea, da = head_terms(lane0 + 2 * k, g)
            gb, bwb, eb, db = head_terms(lane0 + 2 * k + 1, g)
            s_in = st_ref[0, k]
            y = (_dot(jnp.concatenate([ga, gb], axis=1), rhs)
                 + _dot(c_g[g], s_in.astype(BF16)) * jnp.where(lo, ea, eb))
            st_ref[0, k] = s_in * jnp.where(lo_row, da, db) + _dot(jnp.concatenate([bwa, bwb], axis=1), rhs)
            sl = slice(k * LANES, (k + 1) * LANES)
            if reverse:
                zt = z_ref[0, rows, sl]
                gat_ref[sub, :, sl] = (yf_ref[0, rows, sl] + y + dsk_ref[:, sl] * xpair) * _silu(zt)
            else:
                y_ref[0, rows, sl] = y

        if reverse:
            gw = SSD_DI // SSD_G
            for g in range(SSD_G):
                seg = gat_ref[sub, :, g * gw:(g + 1) * gw]
                y_ref[0, rows, g * gw:(g + 1) * gw] = (_rms(seg) * nrm_ref[:, g * gw:(g + 1) * gw]).astype(BF16)


def _conv_shift_matrices():
    taps = [k for k in range(D_CONV) if k != D_CONV // 2]
    out = np.zeros((len(taps) * CHUNK, CHUNK + 2 * HALO), np.float32)
    for idx, k in enumerate(taps):
        out[idx * CHUNK + np.arange(CHUNK), np.arange(CHUNK) + HALO - D_CONV // 2 + k] = 1.0
    return out


_CONV_SHIFTS = _conv_shift_matrices()


def _ssd(xbc, dtm, h0, w, reverse, merge=None):
    bsz, n, _ = xbc.shape
    rows = SSD_SUBS * CHUNK
    nc = n // rows
    per = rows // HALO
    nhb = n // HALO
    cidx = (lambda c: nc - 1 - c) if reverse else (lambda c: c)
    tok = lambda width: pl.BlockSpec((1, rows, width), lambda b, c: (b, cidx(c), 0))
    state_spec = pl.BlockSpec((1, SSD_HEADS // 2, SSD_N, LANES), lambda b, c: (b, 0, 0, 0))
    a_row = w["a_rev" if reverse else "a_fwd"]
    dt_next = pl.BlockSpec((1, rows, LANES), lambda b, c: (b, cidx(jnp.minimum(c + 1, nc - 1)), 0))
    dec_scratch = pltpu.VMEM((2, SSD_SUBS, 3, CHUNK, LANES), F32)
    y_spec = [tok(SSD_DI), state_spec]
    y_shape = [jax.ShapeDtypeStruct((bsz, n, SSD_DI), BF16 if reverse else F32), jax.ShapeDtypeStruct(h0.shape, F32)]
    if reverse:
        y_fwd, z = merge
        in_specs = [tok(CONV_CH), tok(LANES), dt_next, _const_spec(w["dt_bias"].shape), _const_spec(a_row.shape),
                    state_spec, tok(SSD_DI), tok(SSD_DI), _const_spec(w["d_skip"].shape),
                    _const_spec(w["ssd_norm"].shape)]
        args = [xbc, dtm, dtm, w["dt_bias"], a_row, h0, y_fwd, z, w["d_skip"], w["ssd_norm"]]
        scratch = [dec_scratch, pltpu.VMEM((SSD_SUBS, CHUNK, SSD_DI), F32)]
    else:
        in_specs = [
            tok(CONV_CH),
            pl.BlockSpec((1, HALO, CONV_CH), lambda b, c: (b, jnp.maximum(c * per - 1, 0), 0)),
            pl.BlockSpec((1, HALO, CONV_CH), lambda b, c: (b, jnp.minimum((c + 1) * per, nhb - 1), 0)),
            tok(LANES),
            dt_next,
            _const_spec(w["conv_w"].shape),
            _const_spec(w["conv_b"].shape),
            _const_spec(_CONV_SHIFTS.shape),
            _const_spec(w["dt_bias"].shape),
            _const_spec(a_row.shape),
            state_spec,
        ]
        args = [xbc, xbc, xbc, dtm, dtm, w["conv_w"], w["conv_b"], jnp.asarray(_CONV_SHIFTS, BF16), w["dt_bias"],
                a_row, h0]
        scratch = [dec_scratch]
        y_spec.append(tok(CONV_CH))
        y_shape.append(jax.ShapeDtypeStruct((bsz, n, CONV_CH), F32))
    return pl.pallas_call(
        functools.partial(_ssd_kernel, nb=nc, reverse=reverse, lane0=SSD_HEADS if reverse else 0),
        grid=(bsz, nc),
        in_specs=in_specs,
        out_specs=y_spec,
        out_shape=y_shape,
        scratch_shapes=scratch,
        compiler_params=_params("arbitrary", "arbitrary"),
        name="ssd_rev" if reverse else "ssd_fwd",
    )(*args)


def _flash_kernel(*refs, n_steps, has_ctx):
    if has_ctx:
        (qt_ref, k_ref, vt_ref, kc_ref, vct_ref, o_ref,
         s_ref, p_ref, al_ref, mx_ref, m_ref, acc_ref) = refs
    else:
        qt_ref, k_ref, vt_ref, o_ref, s_ref, p_ref, al_ref, mx_ref, m_ref, acc_ref = refs
    tk = vt_ref.shape[-1]
    dv = vt_ref.shape[-2]
    total = n_steps + (1 if has_ctx else 0)
    qt = qt_ref[0, 0]
    m_ref[...] = jnp.full(m_ref.shape, -jnp.inf, F32)
    acc_ref[...] = jnp.zeros(acc_ref.shape, F32)

    def is_ctx(step):
        return has_ctx and isinstance(step, int) and step == n_steps

    def rows_of(step):
        return kc_ref.shape[2] if is_ctx(step) else tk

    def scores(step, slot):
        if is_ctx(step):
            s = _dot(kc_ref[0, 0], qt)
        else:
            start = step * tk if isinstance(step, int) else pl.multiple_of(step * tk, tk)
            s = _dot(k_ref[0, 0, pl.ds(start, tk), :], qt)
        s_ref[slot, 0:rows_of(step)] = s
        mx_ref[slot] = jnp.max(s, axis=0, keepdims=True)

    def softmax(step, slot):
        rows = rows_of(step)
        s = s_ref[slot, 0:rows]
        m_old = m_ref[...]
        m_new = jnp.maximum(m_old, mx_ref[slot])
        m_ref[...] = m_new
        al_ref[slot] = jnp.exp2(m_old - m_new)
        p_ref[slot, 0:rows] = jnp.exp2(s - m_new).astype(BF16)

    def values(step, slot):
        vt = vct_ref[0, 0, 0] if is_ctx(step) else vt_ref[0, 0, step]
        vt1 = jnp.concatenate([vt, jnp.ones((BF16_ROWS, vt.shape[1]), BF16)], axis=0)
        acc_ref[...] = al_ref[slot] * acc_ref[...] + _dot(vt1, p_ref[slot, 0:rows_of(step)])

    def tick(t, parity):
        static = isinstance(t, int)
        if not static or 0 <= t - 2 * FLASH_LAG < total:
            values(t - 2 * FLASH_LAG, parity)
        if not static or 0 <= t - FLASH_LAG < total:
            softmax(t - FLASH_LAG, parity)
        if not static or t < total:
            scores(t, parity)

    first = 2 * FLASH_LAG
    n_groups = max(n_steps - first, 0) // FLASH_GROUP
    for t in range(first):
        tick(t, t % 2)
    if n_groups > 0:
        def group(i, carry):
            for u in range(FLASH_GROUP):
                tick(first + FLASH_GROUP * i + u, u % 2)
            return carry

        lax.fori_loop(0, n_groups, group, 0)
    for t in range(first + FLASH_GROUP * n_groups, total + 2 * FLASH_LAG):
        tick(t, t % 2)
    o_ref[0] = (acc_ref[0:dv, :] / acc_ref[dv:dv + 1, :]).T.astype(BF16)


def _flash(qt, k, vt, ctx_kv=None):
    bsz, nh, dqk, n = qt.shape
    n_steps, dv, tk = vt.shape[2:]
    tq = min(FLASH_TQ, n)
    assert n % tq == 0
    has_ctx = ctx_kv is not None
    whole = lambda x: pl.BlockSpec((1, 1) + x.shape[2:], lambda b, h, i: (b, h) + (0,) * (x.ndim - 2))
    in_specs = [pl.BlockSpec((1, 1, dqk, tq), lambda b, h, i: (b, h, 0, i)), whole(k), whole(vt)]
    args = [qt, k, vt]
    if has_ctx:
        kc, vct = ctx_kv
        assert vct.shape[2] == 1 and kc.shape[2] <= tk
        in_specs += [whole(kc), whole(vct)]
        args += [kc, vct]
    return pl.pallas_call(
        functools.partial(_flash_kernel, n_steps=n_steps, has_ctx=has_ctx),
        grid=(bsz, nh, n // tq),
        in_specs=in_specs,
        out_specs=pl.BlockSpec((1, tq, dv), lambda b, h, i: (b, i, h)),
        out_shape=jax.ShapeDtypeStruct((bsz, n, nh * dv), BF16),
        scratch_shapes=[
            pltpu.VMEM((2, tk, tq), F32), pltpu.VMEM((2, tk, tq), BF16),
            pltpu.VMEM((2, 1, tq), F32), pltpu.VMEM((2, 1, tq), F32),
            pltpu.VMEM((1, tq), F32), pltpu.VMEM((dv + BF16_ROWS, tq), F32),
        ],
        compiler_params=_params("arbitrary", "arbitrary", "arbitrary"),
        name="mla_flash",
    )(*args)


def _odd_proj_kernel(h_ref, m_ref, cos_ref, sin_ref, cost_ref, sint_ref, wq_ref, wk_ref, wv_ref, gqt_ref,
                     gk_ref, qt_ref, k_ref, vt_ref):
    u = _modulate(h_ref[0], m_ref[0, 3:4, :], m_ref[0, 4:5, :]).astype(BF16)
    cos = cos_ref[...]
    sin = sin_ref[...]
    cost = cost_ref[...]
    sint = sint_ref[...]
    gqt = gqt_ref[...]
    qd = GQA_HD // 4
    lo = _lane_lo(cos.shape)
    first_quarter = (lax.broadcasted_iota(jnp.int32, cos.shape, 1) % (GQA_HD // 2)) < (GQA_HD // 4)

    def norm_rope(x2, g2):
        sq = x2 * x2
        s_lo = jnp.sum(jnp.where(lo, sq, 0.0), axis=-1, keepdims=True) * (1.0 / GQA_HD)
        s_hi = jnp.sum(jnp.where(lo, 0.0, sq), axis=-1, keepdims=True) * (1.0 / GQA_HD)
        y = x2 * jnp.where(lo, lax.rsqrt(s_lo + EPS), lax.rsqrt(s_hi + EPS)) * g2
        rot = jnp.where(first_quarter, pltpu.roll(y, LANES - GQA_HD // 4, 1), pltpu.roll(y, GQA_HD // 4, 1))
        return y * cos + rot * sin

    q = _dot(u, wq_ref[...])
    k = _dot(u, wk_ref[...])
    v = _dot(u, wv_ref[...])
    for j in range(q.shape[1] // LANES):
        qt = q[:, j * LANES:(j + 1) * LANES].T
        for half in range(2):
            row0 = half * GQA_HD
            y = qt[row0:row0 + GQA_HD]
            y = y * lax.rsqrt(jnp.mean(y * y, axis=0, keepdims=True) + EPS) * gqt
            rot = jnp.concatenate([y[qd:2 * qd], y[0:qd], y[3 * qd:4 * qd], y[2 * qd:3 * qd]], axis=0)
            out = (y * cost + rot * sint) * GQA_PRESCALE
            qt_ref[0, j * LANES + row0:j * LANES + row0 + GQA_HD, :] = out.astype(BF16)
    for j in range(k.shape[1] // LANES):
        sl = slice(j * LANES, (j + 1) * LANES)
        k_ref[0, :, sl] = norm_rope(k[:, sl], gk_ref[...]).astype(BF16)
        vt_ref[0, sl, :] = v[:, sl].T.astype(BF16)


def _odd_proj(h, mod_l, row_of, cos, sin, cos_t, sin_t, w):
    bsz, n, d = h.shape
    tm = _token_tile(n)
    gqt = jnp.broadcast_to(w["gq"].reshape(GQA_HD, 1), (GQA_HD, tm))
    weights = [w["wq"], w["wk"], w["wv"], gqt, w["gk"]]
    tok = lambda width: pl.BlockSpec((1, tm, width), lambda b, i: (b, i, 0))
    nq = GQA_HQ * GQA_HD
    nkv = GQA_HKV * GQA_HD
    return pl.pallas_call(
        _odd_proj_kernel,
        grid=(bsz, n // tm),
        in_specs=[
            tok(d),
            pl.BlockSpec((1, N_MOD, d), lambda b, i: (row_of(b), 0, 0)),
            pl.BlockSpec((tm, LANES), lambda b, i: (i, 0)),
            pl.BlockSpec((tm, LANES), lambda b, i: (i, 0)),
            pl.BlockSpec((GQA_HD, tm), lambda b, i: (0, i)),
            pl.BlockSpec((GQA_HD, tm), lambda b, i: (0, i)),
        ] + [_const_spec(x.shape) for x in weights],
        out_specs=[
            pl.BlockSpec((1, nq, tm), lambda b, i: (b, 0, i)),
            tok(nkv),
            pl.BlockSpec((1, nkv, tm), lambda b, i: (b, 0, i)),
        ],
        out_shape=[
            jax.ShapeDtypeStruct((bsz, nq, n), BF16),
            jax.ShapeDtypeStruct((bsz, n, nkv), BF16),
            jax.ShapeDtypeStruct((bsz, nkv, n), BF16),
        ],
        compiler_params=_params("arbitrary", "arbitrary"),
        name="odd_proj",
    )(h, mod_l, cos, sin, cos_t, sin_t, *weights)


def _band_bias(tq):
    j = np.arange(tq + 2 * WINDOW)[:, None]
    t = (np.arange(2 * GQA_REP * tq) % tq)[None, :]
    return np.where(np.abs(j - WINDOW - t) <= WINDOW, 0.0, -np.inf).astype(np.float32)


def _window_kernel(*refs, nb, local):
    if local:
        (sink_ref, qt_ref, kp_ref, kc_ref, kn_ref, vp_ref, vc_ref, vn_ref, kx_ref, vx_ref, bias_ref,
         o_ref) = refs
    else:
        sink_ref, qt_ref, kx_ref, vx_ref, o_ref = refs
    i = pl.program_id(1)
    tq = qt_ref.shape[2]
    cols = GQA_REP * tq
    top = lax.broadcasted_iota(jnp.int32, (LANES, cols), 0) < GQA_HD
    if local:
        span = tq + 2 * WINDOW
        ninf = jnp.float32(-jnp.inf)
        bias = jnp.concatenate([
            jnp.where(i > 0, bias_ref[0:WINDOW, :], ninf),
            bias_ref[WINDOW:WINDOW + tq, :],
            jnp.where(i < nb - 1, bias_ref[WINDOW + tq:span, :], ninf)], axis=0)
    for gp in range(GQA_HKV // 2):
        fsl = slice(gp * LANES, (gp + 1) * LANES)
        if local:
            k_all = jnp.concatenate(
                [kp_ref[0, :, fsl], kc_ref[0, :, fsl], kn_ref[0, :, fsl], kx_ref[0, :, fsl]], axis=0)
            vt_all = jnp.concatenate(
                [vp_ref[0, fsl, :], vc_ref[0, fsl, :], vn_ref[0, fsl, :], vx_ref[0, fsl, :]], axis=1)
        else:
            k_all = kx_ref[0, :, fsl]
            vt_all = vx_ref[0, fsl, :]
        vt1 = jnp.concatenate([vt_all, jnp.ones((BF16_ROWS, vt_all.shape[1]), BF16)], axis=0)
        qs = jnp.concatenate(
            [qt_ref[0, (gp * GQA_REP + r) * LANES:(gp * GQA_REP + r + 1) * LANES, :] for r in range(GQA_REP)],
            axis=1)
        q2 = jnp.concatenate([jnp.where(top, qs, jnp.zeros_like(qs)), jnp.where(top, jnp.zeros_like(qs), qs)], axis=1)
        sink = jnp.concatenate(
            [jnp.full((1, tq), sink_ref[(2 * gp + half) * GQA_REP + r] * LOG2E, F32)
             for half in range(2) for r in range(GQA_REP)], axis=1)
        s = _dot(k_all, q2)
        if local:
            parts = [s[0:span] + bias, s[span:]]
        else:
            parts = [s]
        m = sink
        for part in parts:
            m = jnp.maximum(m, jnp.max(part, axis=0, keepdims=True))
        p = jnp.concatenate([jnp.exp2(part - m).astype(BF16) for part in parts], axis=0)
        acc = _dot(vt1, p)
        o2 = acc[0:LANES] / (acc[LANES:LANES + 1] + jnp.exp2(sink - m))
        ot = jnp.where(top, o2[:, 0:cols], o2[:, cols:2 * cols])
        for r in range(GQA_REP):
            col = gp * GQA_REP + r
            o_ref[0, :, col * LANES:(col + 1) * LANES] = ot[:, r * tq:(r + 1) * tq].T.astype(BF16)


def _window_attn(qt, kx, vxt, sink, local_kv=None):
    bsz, nq, n = qt.shape
    nkv = kx.shape[2]
    tq = WINDOW
    nb = n // tq
    local = local_kv is not None
    in_specs = [pl.BlockSpec(memory_space=pltpu.SMEM), pl.BlockSpec((1, nq, tq), lambda b, i: (b, 0, i))]
    args = [sink, qt]
    if local:
        k, vt = local_kv
        prev = lambda i: jnp.maximum(i - 1, 0)
        nxt = lambda i: jnp.minimum(i + 1, nb - 1)
        in_specs += [pl.BlockSpec((1, tq, nkv), lambda b, i: (b, prev(i), 0)),
                     pl.BlockSpec((1, tq, nkv), lambda b, i: (b, i, 0)),
                     pl.BlockSpec((1, tq, nkv), lambda b, i: (b, nxt(i), 0)),
                     pl.BlockSpec((1, nkv, tq), lambda b, i: (b, 0, prev(i))),
                     pl.BlockSpec((1, nkv, tq), lambda b, i: (b, 0, i)),
                     pl.BlockSpec((1, nkv, tq), lambda b, i: (b, 0, nxt(i)))]
        args += [k, k, k, vt, vt, vt]
    in_specs += [pl.BlockSpec((1,) + kx.shape[1:], lambda b, i: (b, 0, 0)),
                 pl.BlockSpec((1,) + vxt.shape[1:], lambda b, i: (b, 0, 0))]
    args += [kx, vxt]
    if local:
        bias = jnp.asarray(_band_bias(tq))
        in_specs.append(_const_spec(bias.shape))
        args.append(bias)
    return pl.pallas_call(
        functools.partial(_window_kernel, nb=nb, local=local),
        grid=(bsz, nb),
        in_specs=in_specs,
        out_specs=pl.BlockSpec((1, tq, nq), lambda b, i: (b, i, 0)),
        out_shape=jax.ShapeDtypeStruct((bsz, n, nq), BF16),
        compiler_params=_params("arbitrary", "arbitrary"),
        name="window_attn" if local else "sink_attn",
    )(*args)


_ROT_SRC = np.concatenate([np.arange(16, 32), np.arange(0, 16), np.arange(48, 64), np.arange(32, 48)])
_ROT_SIGN = np.concatenate([-np.ones(16), np.ones(16), -np.ones(16), np.ones(16)]).astype(np.float32)


def _rope_tables(n):
    quarter = MLA_DR // 4
    freqs = ROPE_BASE ** (-jnp.arange(quarter, dtype=F32) / quarter)
    t = jnp.arange(n)
    row = (t // GRID_W).astype(F32)
    col = (t % GRID_W).astype(F32)
    ang_r = row[:, None] * freqs
    ang_c = col[:, None] * freqs
    cos = jnp.concatenate([jnp.cos(ang_r), jnp.cos(ang_r), jnp.cos(ang_c), jnp.cos(ang_c)], axis=-1)
    sin = jnp.concatenate([jnp.sin(ang_r), jnp.sin(ang_r), jnp.sin(ang_c), jnp.sin(ang_c)], axis=-1)
    return cos, sin


def _ffn_weights(w_in, w_out):
    d = w_in.shape[0]
    wg = w_in[:, :FFN_DIM].reshape(d, N_FFN_CHUNKS, FFN_CHUNK).transpose(1, 0, 2).astype(BF16)
    wu = w_in[:, FFN_DIM:].reshape(d, N_FFN_CHUNKS, FFN_CHUNK).transpose(1, 0, 2).astype(BF16)
    return wg, wu, w_out.astype(BF16)


def _pad_lanes(row, offset=0):
    out = jnp.zeros((1, LANES), F32)
    return out.at[0, offset:offset + row.shape[0]].set(row.astype(F32))


def _even_weights(w_in, conv_w, conv_b, dt_bias, a_log, d_skip, ssd_norm, qa_norm, w_qb, kv_norm,
                  w_kvb, qn_g, qr_g, kn_g, kr_g, w_out):
    d = w_in.shape[0]
    sign = jnp.asarray(_ROT_SIGN)
    w_dt = w_in[:, _E1:_E2]
    w_kr = w_in[:, _E4:]
    wm = jnp.concatenate([w_dt, jnp.zeros((d, LANES - 2 * SSD_HEADS), F32), w_kr, w_kr[:, _ROT_SRC] * sign], axis=1)
    wq = w_qb.reshape(MLA_Q_RANK, MLA_HEADS, MLA_DN + MLA_DR)
    wqr = wq[:, :, MLA_DN:].reshape(MLA_Q_RANK, MLA_HEADS * MLA_DR)
    wkv = w_kvb.reshape(MLA_KV_RANK, MLA_HEADS, MLA_DN + MLA_DV)
    row = lambda g: g.reshape(1, -1).astype(F32)
    return {
        "wz": w_in[:, :_E0].astype(BF16),
        "wx": w_in[:, _E0:_E1].astype(BF16),
        "wm": wm.astype(BF16),
        "wqa": w_in[:, _E2:_E3].astype(BF16),
        "wkv": w_in[:, _E3:_E4].astype(BF16),
        "wqn": wq[:, :, :MLA_DN].reshape(MLA_Q_RANK, MLA_HEADS * MLA_DN).astype(BF16),
        "wqr": wqr.astype(BF16),
        "wkn": wkv[:, :, :MLA_DN].reshape(MLA_KV_RANK, MLA_HEADS * MLA_DN).astype(BF16),
        "wv": wkv[:, :, MLA_DN:].reshape(MLA_KV_RANK, MLA_HEADS * MLA_DV).astype(BF16),
        "gqa": row(qa_norm), "gkv": row(kv_norm), "gqn": row(qn_g), "gkn": row(kn_g),
        "gqr": row(qr_g),
        "gkr": row(jnp.concatenate([kr_g, kr_g[_ROT_SRC]])),
        "conv_w": conv_w.astype(F32), "conv_b": row(conv_b),
        "dt_bias": _pad_lanes(dt_bias.reshape(-1)),
        "a_fwd": _pad_lanes(-jnp.exp(a_log[0].astype(F32)), 0),
        "a_rev": _pad_lanes(-jnp.exp(a_log[1].astype(F32)), SSD_HEADS),
        "d_skip": row(jnp.repeat(d_skip, SSD_P)), "ssd_norm": row(ssd_norm),
        "wo_ssd": w_out[:SSD_DI].astype(BF16), "wo_mla": w_out[SSD_DI:].astype(BF16),
    }


def _gqa_col_perm():
    perm = np.zeros(GQA_HQ * GQA_HD, np.int32)
    for gp in range(GQA_HKV // 2):
        for r in range(GQA_REP):
            for half in range(2):
                head = (2 * gp + half) * GQA_REP + r
                new = ((gp * GQA_REP + r) * 2 + half) * GQA_HD
                perm[new:new + GQA_HD] = np.arange(head * GQA_HD, (head + 1) * GQA_HD)
    return perm


_GQA_PERM = _gqa_col_perm()


def _odd_weights(w_in, q_g, k_g, sink, w_out):
    nq = GQA_HQ * GQA_HD
    nkv = GQA_HKV * GQA_HD
    tile2 = lambda g: jnp.concatenate([g, g]).reshape(1, LANES).astype(F32)
    return {
        "wq": w_in[:, :nq][:, _GQA_PERM].astype(BF16),
        "wk": w_in[:, nq:nq + nkv].astype(BF16),
        "wv": w_in[:, nq + nkv:].astype(BF16),
        "gq": q_g.astype(F32), "gk": tile2(k_g),
        "sink": sink.astype(F32),
        "wo": w_out[_GQA_PERM, :].astype(BF16),
    }


def kernel(x, c, ctx, c_ctx, w_mod, b_mod, w_ff1_in, w_ff1_out, w_ff2_in, w_ff2_out, w_in_e, conv_w, conv_b, dt_bias, a_log, d_skip, ssd_norm, mla_qa_norm, w_qb, mla_kv_norm, w_kvb, mla_qn_norm, mla_qr_norm, mla_kn_norm, mla_kr_norm, w_out_e, w_in_o, gqa_q_norm, gqa_k_norm, sink, w_out_o):
    bsz, n, d = x.shape
    lc = ctx.shape[1]
    depth = w_mod.shape[0]
    assert d == D_MODEL and bsz + 1 <= MOD_ROWS and n % TOKEN_TILE == 0 and lc % CHUNK == 0

    cond = jnp.zeros((MOD_ROWS, d), F32).at[:bsz].set(c).at[bsz].set(c_ctx)
    mod = _mod_vectors(cond, w_mod, b_mod).reshape(depth, MOD_ROWS, N_MOD, d)
    lat_row = lambda b: b
    ctx_row = lambda b: bsz

    cos64, sin64 = _rope_tables(n)
    tab_lat = jnp.concatenate([cos64, sin64], axis=-1)
    tab_ctx = jnp.concatenate([jnp.ones((lc, MLA_DR), F32), jnp.zeros((lc, MLA_DR), F32)], axis=-1)
    sign = jnp.asarray(_ROT_SIGN)
    cos_lat = jnp.concatenate([cos64, cos64], axis=-1)
    sin_lat = jnp.concatenate([sin64 * sign, sin64 * sign], axis=-1)
    cos_ctx = jnp.ones((lc, LANES), F32)
    sin_ctx = jnp.zeros((lc, LANES), F32)
    cos_t_lat = cos64.T
    sin_t_lat = (sin64 * sign).T
    cos_t_ctx = jnp.ones((GQA_HD, lc), F32)
    sin_t_ctx = jnp.zeros((GQA_HD, lc), F32)

    h, hc = x, ctx
    for l in range(depth):
        need_ctx = l < depth - 1
        mod_l = mod[l]
        ff1 = _ffn_weights(w_ff1_in[l], w_ff1_out[l])
        ff2 = _ffn_weights(w_ff2_in[l], w_ff2_out[l])
        h = _ffn(h, mod_l, lat_row, *ff1, k0=0)
        hc = _ffn(hc, mod_l, ctx_row, *ff1, k0=0)
        if l % 2 == 0:
            e = l // 2
            w = _even_weights(w_in_e[e], conv_w[e], conv_b[e], dt_bias[e], a_log[e], d_skip[e], ssd_norm[e],
                              mla_qa_norm[e], w_qb[e], mla_kv_norm[e], w_kvb[e], mla_qn_norm[e],
                              mla_qr_norm[e], mla_kn_norm[e], mla_kr_norm[e], w_out_e[e])
            zc, xbcc, dtc, qc, kc, vc = _even_proj(hc, mod_l, ctx_row, tab_ctx, cos_t_ctx, sin_t_ctx, w)
            z, xbc, dtm, q, k, v = _even_proj(h, mod_l, lat_row, tab_lat, cos_t_lat, sin_t_lat, w)
            s0 = jnp.zeros((bsz, SSD_HEADS // 2, SSD_N, LANES), F32)
            yfc, s_fc, xac = _ssd(xbcc, dtc, s0, w, reverse=False)
            ysc, s_bc = _ssd(xac, dtc, s0, w, reverse=True, merge=(yfc, zc))
            yf, _, xa = _ssd(xbc, dtm, s_fc, w, reverse=False)
            ys, _ = _ssd(xa, dtm, s_bc, w, reverse=True, merge=(yf, z))
            o = _flash(q, k, v, ctx_kv=(kc, vc))
            mix = ((ys, w["wo_ssd"]), (o, w["wo_mla"]))
            if need_ctx:
                mix_c = ((ysc, w["wo_ssd"]), (_flash(qc, kc, vc), w["wo_mla"]))
        else:
            o_ = l // 2
            w = _odd_weights(w_in_o[o_], gqa_q_norm[o_], gqa_k_norm[o_], sink[o_], w_out_o[o_])
            qc, kc, vc = _odd_proj(hc, mod_l, ctx_row, cos_ctx, sin_ctx, cos_t_ctx, sin_t_ctx, w)
            q, k, v = _odd_proj(h, mod_l, lat_row, cos_lat, sin_lat, cos_t_lat, sin_t_lat, w)
            mix = ((_window_attn(q, kc, vc, w["sink"], local_kv=(k, v)), w["wo"]),)
            if need_ctx:
                mix_c = ((_window_attn(qc, kc, vc, w["sink"]), w["wo"]),)
        h = _ffn(h, mod_l, lat_row, *ff2, k0=6, mix=mix)
        if need_ctx:
            hc = _ffn(hc, mod_l, ctx_row, *ff2, k0=6, mix=mix_c)
    return h
```
